```python
import jax, jax.numpy as jnp
from jax import lax
import numpy as np

D_MODEL = 4096
BATCH = 1
SEQ = 8192
DEPTH = 2
DEC_BATCH = 32
DEC_SEQ = 32
PAST_LEN = 1024

CHUNK = 64
HEAD_DIM = 128
CONV_CH = D_MODEL // 4
N_FOX_HEADS = (D_MODEL - CONV_CH) // (2 * HEAD_DIM)
N_HGRN_HEADS = N_FOX_HEADS
FOX_W = N_FOX_HEADS * HEAD_DIM
HGRN_DK = HEAD_DIM
HGRN_DV = HEAD_DIM
HGRN_W = N_HGRN_HEADS * HGRN_DV
D_MIX = CONV_CH + FOX_W + HGRN_W
CONV_WIDTH = 31
Q_BLOCK = 128
N_GROUPS = 4
EXPERTS_PER_GROUP = 4
TOP_K = 2
D_FF_EXPERT = D_MODEL // 4
FOX_FORGET_BIAS = 3.0
EPS = 1e-6
PROJ_SIZES = (CONV_CH, CONV_CH, FOX_W, FOX_W, FOX_W, N_FOX_HEADS,
              N_HGRN_HEADS * HGRN_DK, N_HGRN_HEADS * HGRN_DK, HGRN_W, HGRN_W)
N_IN = sum(PROJ_SIZES)

kernel_name = 'hymba_conv_fox_hgrn2_hiermoe_stream_step'

F32 = jnp.float32


def _rmsnorm(x, g):
    xf = x.astype(F32)
    y = xf * lax.rsqrt(jnp.mean(xf * xf, axis=-1, keepdims=True) + EPS)
    return (y * g.astype(F32)).astype(x.dtype)


def _layernorm(x, g, b):
    xf = x.astype(F32)
    mu = jnp.mean(xf, axis=-1, keepdims=True)
    var = jnp.mean(jnp.square(xf - mu), axis=-1, keepdims=True)
    y = (xf - mu) * lax.rsqrt(var + EPS)
    return (y * g.astype(F32) + b.astype(F32)).astype(x.dtype)


def _conv_module(a, buf, w_dw, b_dw, ln_g, ln_b, w_pw):
    xp = jnp.concatenate([buf.astype(a.dtype), a], axis=1)
    y = lax.conv_general_dilated(xp, w_dw[:, None, :].astype(xp.dtype), (1,), 'VALID',
                                 dimension_numbers=('NWC', 'WIO', 'NWC'),
                                 feature_group_count=CONV_CH)
    y = jax.nn.silu(_layernorm(y + b_dw.astype(y.dtype), ln_g, ln_b))
    return y @ w_pw.astype(y.dtype), xp[:, -(CONV_WIDTH - 1):]


def _fox_attention(q, k_all, v_all, logf_all):
    B, T, H, Dh = q.shape
    S = k_all.shape[1]
    P = S - T
    c_k = jnp.transpose(jnp.cumsum(logf_all.astype(F32), axis=1), (0, 2, 1))
    c_q = c_k[:, :, P:]
    blk = min(Q_BLOCK, T)
    nb = T // blk
    q_b = jnp.moveaxis(q.reshape(B, nb, blk, H, Dh), 1, 0)
    cq_b = jnp.moveaxis(c_q.reshape(B, H, nb, blk), 2, 0)
    pos_b = (P + jnp.arange(T)).reshape(nb, blk)
    k_pos = jnp.arange(S)
    scale = Dh ** -0.5

    def block(args):
        qi, ci, pi = args
        s = jnp.einsum('bqhd,bkhd->bhqk', qi, k_all).astype(F32) * scale
        s = s + ci[..., None] - c_k[:, :, None, :]
        mask = k_pos[None, :] <= pi[:, None]
        s = jnp.where(mask[None, None], s, -jnp.inf)
        p = jax.nn.softmax(s, axis=-1)
        return jnp.einsum('bhqk,bkhd->bqhd', p.astype(v_all.dtype), v_all)

    o = lax.map(block, (q_b, cq_b, pos_b))
    return jnp.moveaxis(o, 0, 1).reshape(B, T, H, Dh)


def _hgrn_recurrence(q, k, v, logf, S0):
    B, T, H, K = q.shape
    V = v.shape[-1]
    L = min(CHUNK, T)
    n = T // L

    def chunks(z):
        return jnp.moveaxis(z.reshape((B, n, L) + z.shape[2:]), 1, 0)

    causal = jnp.tril(jnp.ones((L, L), dtype=bool))[None, :, :, None, None]

    def step(S, xs):
        qc, kc, vc, lc = xs
        b = jnp.cumsum(lc, axis=1)
        o_inter = jnp.einsum('blhk,bhkv->blhv', qc * jnp.exp(b), S)
        diff = b[:, :, None] - b[:, None, :]
        dec = jnp.exp(jnp.where(causal, diff, -jnp.inf))
        A = jnp.einsum('bthk,bshk,btshk->bhts', qc, kc, dec)
        o_intra = jnp.einsum('bhts,bshv->bthv', A, vc)
        b_last = b[:, -1]
        S = jnp.exp(b_last)[..., None] * S + jnp.einsum(
            'bshk,bshv->bhkv', kc * jnp.exp(b_last[:, None] - b), vc)
        return S, o_inter + o_intra

    S_T, o = lax.scan(step, S0, (chunks(q), chunks(k), chunks(v), chunks(logf)))
    return jnp.moveaxis(o, 0, 1).reshape(B, T, H, V), S_T


def _hgrn_mixer(hq, hf, hi, hg, S0, lb, gnorm):
    B, T, _ = hq.shape
    shp = (B, T, N_HGRN_HEADS, -1)
    lbh = lb.reshape(N_HGRN_HEADS, HGRN_DK)
    f = lbh + (1.0 - lbh) * jax.nn.sigmoid(hf.astype(F32).reshape(shp))
    logf = jnp.log(f)
    k = 1.0 - f
    o, S_T = _hgrn_recurrence(hq.astype(F32).reshape(shp), k, hi.astype(F32).reshape(shp),
                              logf, S0.astype(F32))
    o = _rmsnorm(o, gnorm) * jax.nn.silu(hg.astype(F32).reshape(shp))
    return o.reshape(B, T, HGRN_W).astype(hq.dtype), S_T


def _hier_moe(h, w_rg, b_rg, w_re, b_re, w1, w3, w2):
    B, T, D = h.shape
    hn = h.reshape(B * T, D)
    g_logit = jnp.einsum('nd,dg->ng', hn, w_rg).astype(F32) + b_rg.astype(F32)
    g_idx = jnp.argmax(g_logit, axis=-1)
    p_group = jnp.max(jax.nn.softmax(g_logit, axis=-1), axis=-1)
    e_logit = jnp.einsum('nd,dge->nge', hn, w_re).astype(F32) + b_re.astype(F32)
    e_logit = jnp.take_along_axis(e_logit, g_idx[:, None, None], axis=1)[:, 0]
    top_val, top_idx = lax.top_k(e_logit, TOP_K)
    w_top = jax.nn.softmax(top_val, axis=-1) * p_group[:, None]
    gates = jnp.einsum('nk,nke->ne', w_top,
                       jax.nn.one_hot(top_idx, EXPERTS_PER_GROUP, dtype=F32))
    y = jnp.zeros((B * T, D), F32)
    for g in range(N_GROUPS):
        gate_g = jnp.where((g_idx == g)[:, None], gates, 0.0)
        a = jnp.einsum('nd,edf->nef', hn, w1[g])
        b = jnp.einsum('nd,edf->nef', hn, w3[g])
        act = jax.nn.silu(a) * b * gate_g[..., None].astype(a.dtype)
        y = y + jnp.einsum('nef,efd->nd', act, w2[g]).astype(F32)
    return y.astype(h.dtype).reshape(B, T, D)


def _layer(x, conv_buf, past_k, past_v, past_logf, S0, lb,
           g_mix, w_in, conv_w, conv_b, conv_ln_g, conv_ln_b, conv_pw, fox_bf, hgrn_gnorm,
           w_out, g_ffn, w_rg, b_rg, w_re, b_re, w1, w3, w2):
    B, T, _ = x.shape
    h = _rmsnorm(x, g_mix)
    u = h @ w_in
    points = [int(p) for p in np.cumsum(PROJ_SIZES)[:-1]]
    a_val, a_gate, fq, fk, fv, ff, hq, hf, hi, hg = jnp.split(u, points, axis=-1)
    y_a, new_buf = _conv_module(a_val * jax.nn.sigmoid(a_gate), conv_buf,
                                conv_w, conv_b, conv_ln_g, conv_ln_b, conv_pw)
    heads = (B, T, N_FOX_HEADS, HEAD_DIM)
    fk = fk.reshape(heads)
    fv = fv.reshape(heads)
    logf = jax.nn.log_sigmoid(ff.astype(F32) + fox_bf.astype(F32))
    k_all = jnp.concatenate([past_k.astype(fk.dtype), fk], axis=1)
    v_all = jnp.concatenate([past_v.astype(fv.dtype), fv], axis=1)
    logf_all = jnp.concatenate([past_logf.astype(F32), logf], axis=1)
    y_b = _fox_attention(fq.reshape(heads), k_all, v_all, logf_all).reshape(B, T, FOX_W)
    y_c, S_T = _hgrn_mixer(hq, hf, hi, hg, S0, lb, hgrn_gnorm)
    mix = jnp.concatenate([y_a.astype(x.dtype), y_b.astype(x.dtype), y_c.astype(x.dtype)], axis=-1)
    x = x + mix @ w_out
    x = x + _hier_moe(_rmsnorm(x, g_ffn), w_rg, b_rg, w_re, b_re, w1, w3, w2)
    return x, new_buf, fk, fv, logf, S_T


def setup_inputs(seed: int = 0) -> dict:
    key = jax.random.key(seed)
    ks = jax.random.split(key, 32)
    nrm = lambda k, s: jax.random.normal(k, s, F32)
    G, E, F = N_GROUPS, EXPERTS_PER_GROUP, D_FF_EXPERT
    return {
        'x_prompt': nrm(ks[0], (BATCH, SEQ, D_MODEL)),
        'x_sample': nrm(ks[1], (DEC_BATCH, DEC_SEQ, D_MODEL)),
        'cache_conv': 0.5 * nrm(ks[2], (DEPTH, DEC_BATCH, CONV_WIDTH - 1, CONV_CH)),
        'cache_fox_k': nrm(ks[3], (DEPTH, DEC_BATCH, PAST_LEN, N_FOX_HEADS, HEAD_DIM)),
        'cache_fox_v': nrm(ks[4], (DEPTH, DEC_BATCH, PAST_LEN, N_FOX_HEADS, HEAD_DIM)),
        'cache_fox_logf': jax.nn.log_sigmoid(FOX_FORGET_BIAS + nrm(ks[5], (DEPTH, DEC_BATCH, PAST_LEN, N_FOX_HEADS))),
        'state_hgrn': 0.5 * nrm(ks[6], (DEPTH, DEC_BATCH, N_HGRN_HEADS, HGRN_DK, HGRN_DV)),
        'g_mix': 1.0 + 0.02 * nrm(ks[7], (DEPTH, D_MODEL)),
        'w_in': nrm(ks[8], (DEPTH, D_MODEL, N_IN)) * D_MODEL ** -0.5,
        'conv_w': nrm(ks[9], (DEPTH, CONV_WIDTH, CONV_CH)) * CONV_WIDTH ** -0.5,
        'conv_b': 0.02 * nrm(ks[10], (DEPTH, CONV_CH)),
        'conv_ln_g': 1.0 + 0.02 * nrm(ks[11], (DEPTH, CONV_CH)),
        'conv_ln_b': 0.02 * nrm(ks[12], (DEPTH, CONV_CH)),
        'conv_pw': nrm(ks[13], (DEPTH, CONV_CH, CONV_CH)) * CONV_CH ** -0.5,
        'fox_bf': FOX_FORGET_BIAS + 0.1 * nrm(ks[14], (DEPTH, N_FOX_HEADS)),
        'hgrn_lb_logits': nrm(ks[15], (DEPTH, N_HGRN_HEADS * HGRN_DK)),
        'hgrn_gnorm': 1.0 + 0.02 * nrm(ks[16], (DEPTH, HGRN_DV)),
        'w_out': nrm(ks[17], (DEPTH, D_MIX, D_MODEL)) * D_MIX ** -0.5,
        'g_ffn': 1.0 + 0.02 * nrm(ks[18], (DEPTH, D_MODEL)),
        'w_rg': nrm(ks[19], (DEPTH, D_MODEL, G)) * D_MODEL ** -0.5,
        'b_rg': 0.01 * nrm(ks[20], (DEPTH, G)),
        'w_re': nrm(ks[21], (DEPTH, D_MODEL, G, E)) * D_MODEL ** -0.5,
        'b_re': 0.01 * nrm(ks[22], (DEPTH, G, E)),
        'w1': nrm(ks[23], (DEPTH, G, E, D_MODEL, F)) * D_MODEL ** -0.5,
        'w3': nrm(ks[24], (DEPTH, G, E, D_MODEL, F)) * D_MODEL ** -0.5,
        'w2': nrm(ks[25], (DEPTH, G, E, F, D_MODEL)) * F ** -0.5,
        'g_final': 1.0 + 0.02 * nrm(ks[26], (D_MODEL,)),
    }


def reference(x_prompt, x_sample, cache_conv, cache_fox_k, cache_fox_v, cache_fox_logf, state_hgrn,
              g_mix, w_in, conv_w, conv_b, conv_ln_g, conv_ln_b, conv_pw, fox_bf,
              hgrn_lb_logits, hgrn_gnorm, w_out, g_ffn, w_rg, b_rg, w_re, b_re, w1, w3, w2, g_final):
    p_lb = jax.nn.softmax(hgrn_lb_logits.astype(F32), axis=0)
    lb_all = jnp.cumsum(p_lb, axis=0) - p_lb[0:1]
    Bp = x_prompt.shape[0]
    xp, xs = x_prompt, x_sample
    outs_p = [[], [], [], [], []]
    outs_s = [[], [], [], [], []]
    for l in range(DEPTH):
        params = (g_mix[l], w_in[l], conv_w[l], conv_b[l], conv_ln_g[l], conv_ln_b[l], conv_pw[l],
                  fox_bf[l], hgrn_gnorm[l], w_out[l], g_ffn[l], w_rg[l], b_rg[l], w_re[l], b_re[l],
                  w1[l], w3[l], w2[l])
        xp, cb, k, v, lf, S = _layer(
            xp,
            jnp.zeros((Bp, CONV_WIDTH - 1, CONV_CH), xp.dtype),
            jnp.zeros((Bp, 0, N_FOX_HEADS, HEAD_DIM), xp.dtype),
            jnp.zeros((Bp, 0, N_FOX_HEADS, HEAD_DIM), xp.dtype),
            jnp.zeros((Bp, 0, N_FOX_HEADS), F32),
            jnp.zeros((Bp, N_HGRN_HEADS, HGRN_DK, HGRN_DV), F32),
            lb_all[l], *params)
        for lst, val in zip(outs_p, (cb, k, v, lf, S)):
            lst.append(val)
        xs, cb, k, v, lf, S = _layer(
            xs, cache_conv[l], cache_fox_k[l], cache_fox_v[l], cache_fox_logf[l], state_hgrn[l],
            lb_all[l], *params)
        for lst, val in zip(outs_s, (cb, k, v, lf, S)):
            lst.append(val)
    y_prompt = _rmsnorm(xp, g_final)
    y_sample = _rmsnorm(xs, g_final)
    conv_p, k_p, v_p, logf_p, hgrn_p = [jnp.stack(o, axis=0) for o in outs_p]
    conv_s, k_s, v_s, logf_s, hgrn_s = [jnp.stack(o, axis=0) for o in outs_s]
    return (y_prompt, y_sample, conv_p, k_p, v_p, logf_p, hgrn_p, conv_s, k_s, v_s, logf_s, hgrn_s)
```

```python
import functools

import jax
import jax.numpy as jnp
from jax import lax
from jax.experimental import pallas as pl
from jax.experimental.pallas import tpu as pltpu

F32 = jnp.float32
BF16 = jnp.bfloat16

LANES = 128
HEAD_DIM = 128
CONV_WIDTH = 31
HALO = 32
SUB = 16
CHUNK = 64
N_GROUPS = 4
EXPERTS_PER_GROUP = 4
N_EXPERTS = N_GROUPS * EXPERTS_PER_GROUP
FOX_FORGET_EPS = 1e-6
EPS = 1e-6
VMEM_LIMIT = 48 * 1024 * 1024

MOE_TM = 256
NORM_TM = 256
CUM_CHUNK = 256


def _cparams(sem):
    return pltpu.CompilerParams(dimension_semantics=sem, vmem_limit_bytes=VMEM_LIMIT)


def _sigmoid(x):
    return 1.0 / (1.0 + jnp.exp(-x))


def _split3(x):
    hi = x.astype(BF16)
    r1 = x - hi.astype(F32)
    mid = r1.astype(BF16)
    lo = (r1 - mid.astype(F32)).astype(BF16)
    return hi, mid, lo


def _dot(a, b):
    return jnp.dot(a, b, preferred_element_type=F32)


def _dot_nt(a, b):
    return lax.dot_general(a, b, (((1,), (1,)), ((), ())), preferred_element_type=F32)


def _mm_body(*refs, k_sizes, has_res, n_out):
    n_a = len(k_sizes)
    a_refs = refs[:n_a]
    w_ref = refs[n_a]
    pos = n_a + 1
    res_ref = None
    if has_res:
        res_ref = refs[pos]
        pos += 1
    out_refs = refs[pos:pos + n_out]
    acc = None
    off = 0
    for a_ref, ks in zip(a_refs, k_sizes):
        part = _dot(a_ref[...], w_ref[off:off + ks, :])
        acc = part if acc is None else acc + part
        off += ks
    if has_res:
        acc = acc + res_ref[...]
    for o_ref in out_refs:
        o_ref[...] = acc.astype(o_ref.dtype)


def _matmul(a_list, w, out_dtypes, residual=None, tm=1024, tn=512, name="mm"):
    m = a_list[0].shape[0]
    k_sizes = tuple(a.shape[1] for a in a_list)
    k, n = w.shape
    assert sum(k_sizes) == k and m % tm == 0 and n % tn == 0
    in_specs = [pl.BlockSpec((tm, ks), lambda i, j: (i, 0)) for ks in k_sizes]
    in_specs.append(pl.BlockSpec((k, tn), lambda i, j: (0, j)))
    args = list(a_list) + [w]
    if residual is not None:
        in_specs.append(pl.BlockSpec((tm, tn), lambda i, j: (i, j)))
        args.append(residual)
    out_shape = [jax.ShapeDtypeStruct((m, n), dt) for dt in out_dtypes]
    out_specs = [pl.BlockSpec((tm, tn), lambda i, j: (i, j)) for _ in out_dtypes]
    outs = pl.pallas_call(
        functools.partial(_mm_body, k_sizes=k_sizes, has_res=residual is not None,
                          n_out=len(out_dtypes)),
        grid=(m // tm, n // tn),
        in_specs=in_specs, out_specs=out_specs, out_shape=out_shape,
        compiler_params=_cparams(("parallel", "parallel")), name=name,
    )(*args)
    return outs


def _log_sigmoid(z):
    return jnp.minimum(z, 0.0) - jnp.log1p(jnp.exp(-jnp.abs(z)))


def _route_from_logits(lg):
    g = [lg[:, i:i + 1] for i in range(N_GROUPS)]
    gmax = jnp.maximum(jnp.maximum(g[0], g[1]), jnp.maximum(g[2], g[3]))
    gidx = jnp.where(g[0] == gmax, 0, jnp.where(g[1] == gmax, 1, jnp.where(g[2] == gmax, 2, 3)))
    denom = (jnp.exp(g[0] - gmax) + jnp.exp(g[1] - gmax)) + (jnp.exp(g[2] - gmax) + jnp.exp(g[3] - gmax))
    p_group = 1.0 / denom
    e = []
    for j in range(EXPERTS_PER_GROUP):
        cols = [lg[:, N_GROUPS + gg * EXPERTS_PER_GROUP + j:N_GROUPS + gg * EXPERTS_PER_GROUP + j + 1]
                for gg in range(N_GROUPS)]
        e.append(jnp.where(gidx == 0, cols[0], jnp.where(gidx == 1, cols[1],
                                                         jnp.where(gidx == 2, cols[2], cols[3]))))
    v1 = jnp.maximum(jnp.maximum(e[0], e[1]), jnp.maximum(e[2], e[3]))
    i1 = jnp.where(e[0] == v1, 0, jnp.where(e[1] == v1, 1, jnp.where(e[2] == v1, 2, 3)))
    neg = jnp.float32(-jnp.inf)
    e2 = [jnp.where(i1 == j, neg, e[j]) for j in range(EXPERTS_PER_GROUP)]
    v2 = jnp.maximum(jnp.maximum(e2[0], e2[1]), jnp.maximum(e2[2], e2[3]))
    i2 = jnp.where(e2[0] == v2, 0, jnp.where(e2[1] == v2, 1, jnp.where(e2[2] == v2, 2, 3)))
    t = jnp.exp(v2 - v1)
    w1 = p_group / (1.0 + t)
    w2 = p_group * t / (1.0 + t)
    ex1 = (gidx * EXPERTS_PER_GROUP + i1).astype(F32)
    ex2 = (gidx * EXPERTS_PER_GROUP + i2).astype(F32)
    lane = lax.broadcasted_iota(jnp.int32, lg.shape, 1)
    return jnp.where(lane == 0, ex1, jnp.where(lane == 1, ex2,
                                               jnp.where(lane == 2, w1, jnp.where(lane == 3, w2, 0.0))))


def _norm_body(*refs, combine, small_mode, want_hn, final, tm):
    refs = list(refs)
    pos_ref = refs.pop(0) if combine else None
    x_ref = refs.pop(0)
    if combine:
        route_ref = refs.pop(0)
        ys_ref = refs.pop(0)
    g_ref = refs.pop(0)
    if small_mode is not None:
        wh_ref = refs.pop(0)
        wm_ref = refs.pop(0)
        bias_ref = refs.pop(0)
    if combine:
        xo_ref = refs.pop(0)
    if want_hn:
        hn_ref = refs.pop(0)
    if small_mode is not None:
        small_ref = refs.pop(0)
    if final:
        y_ref = refs.pop(0)
    if combine:
        gbuf = refs.pop(0)
        sem = refs.pop(0)

    x = x_ref[...]
    if combine:
        base = pl.program_id(0) * tm

        def _copy(r, k):
            row = pos_ref[2 * (base + r) + k]
            return pltpu.make_async_copy(ys_ref.at[pl.ds(row, 1), :], gbuf.at[k, pl.ds(r, 1), :], sem.at[k])

        def _issue(r, carry):
            _copy(r, 0).start()
            _copy(r, 1).start()
            return carry

        def _wait(r, carry):
            _copy(r, 0).wait()
            _copy(r, 1).wait()
            return carry

        lax.fori_loop(0, tm, _issue, 0)
        lax.fori_loop(0, tm, _wait, 0)
        rt = route_ref[...]
        x = x + rt[:, 2:3] * gbuf[0] + rt[:, 3:4] * gbuf[1]
        xo_ref[...] = x
    ms = jnp.mean(x * x, axis=-1, keepdims=True)
    hn = x * lax.rsqrt(ms + EPS) * g_ref[...]
    if final:
        y_ref[...] = hn
    if want_hn:
        hn_ref[...] = hn.astype(BF16)
    if small_mode is not None:
        hi = hn.astype(BF16)
        mid = (hn - hi.astype(F32)).astype(BF16)
        sm = _dot(hi, wh_ref[...]) + (_dot(hi, wm_ref[...]) + _dot(mid, wh_ref[...]))
        sm = sm + bias_ref[...]
        if small_mode == "logf":
            small_ref[...] = _log_sigmoid(sm)
        else:
            small_ref[...] = _route_from_logits(sm)


def _norm_call(x, g, *, combine=None, small=None, small_mode=None, want_hn=True, final=False, name="norm"):
    n, d = x.shape
    tm = NORM_TM
    row = lambda i, *_: (i, 0)
    fixed = lambda i, *_: (0, 0)
    in_specs = [pl.BlockSpec((tm, d), row)]
    args = [x]
    scalar_args = []
    scratch = []
    if combine is not None:
        pos, route, ys = combine
        scalar_args = [pos]
        in_specs += [pl.BlockSpec((tm, LANES), row), pl.BlockSpec(memory_space=pl.ANY)]
        args += [route, ys]
        scratch = [pltpu.VMEM((2, tm, d), F32), pltpu.SemaphoreType.DMA((2,))]
    in_specs.append(pl.BlockSpec((1, d), fixed))
    args.append(g.reshape(1, d))
    if small_mode is not None:
        wh, wm, bias = small
        in_specs += [pl.BlockSpec((d, LANES), fixed), pl.BlockSpec((d, LANES), fixed),
                     pl.BlockSpec((1, LANES), fixed)]
        args += [wh, wm, bias]
    out_shape, out_specs = [], []
    if combine is not None:
        out_shape.append(jax.ShapeDtypeStruct((n, d), F32))
        out_specs.append(pl.BlockSpec((tm, d), row))
    if want_hn:
        out_shape.append(jax.ShapeDtypeStruct((n, d), BF16))
        out_specs.append(pl.BlockSpec((tm, d), row))
    if small_mode is not None:
        out_shape.append(jax.ShapeDtypeStruct((n, LANES), F32))
        out_specs.append(pl.BlockSpec((tm, LANES), row))
    if final:
        out_shape.append(jax.ShapeDtypeStruct((n, d), F32))
        out_specs.append(pl.BlockSpec((tm, d), row))
    grid_spec = pltpu.PrefetchScalarGridSpec(
        num_scalar_prefetch=len(scalar_args), grid=(n // tm,),
        in_specs=in_specs, out_specs=out_specs, scratch_shapes=scratch)
    return pl.pallas_call(
        functools.partial(_norm_body, combine=combine is not None, small_mode=small_mode,
                          want_hn=want_hn, final=final, tm=tm),
        grid_spec=grid_spec, out_shape=out_shape,
        compiler_params=_cparams(("arbitrary",)), name=name,
    )(*scalar_args, *args)


def _cumsum_body(x_ref, o_ref, *, n_chunks):
    r = lax.broadcasted_iota(jnp.int32, (CUM_CHUNK, CUM_CHUNK), 0)
    c = lax.broadcasted_iota(jnp.int32, (CUM_CHUNK, CUM_CHUNK), 1)
    tri = (r <= c).astype(BF16)
    carry = jnp.zeros((x_ref.shape[0], 1), F32)
    for ch in range(n_chunks):
        sl = slice(ch * CUM_CHUNK, (ch + 1) * CUM_CHUNK)
        hi, mid, lo = _split3(x_ref[:, sl])
        y = (_dot(hi, tri) + _dot(mid, tri)) + _dot(lo, tri) + carry
        o_ref[:, sl] = y
        carry = y[:, CUM_CHUNK - 1:CUM_CHUNK]


def _cumsum_rows(x, name):
    rows, s = x.shape
    assert s % CUM_CHUNK == 0
    return pl.pallas_call(
        functools.partial(_cumsum_body, n_chunks=s // CUM_CHUNK),
        out_shape=jax.ShapeDtypeStruct((rows, s), F32), name=name,
        compiler_params=pltpu.CompilerParams(vmem_limit_bytes=VMEM_LIMIT),
    )(x)


def _conv_body(av_ref, ag_ref, cache_ref, w_ref, b_ref, lg_ref, lb_ref, pw_ref, y_ref, st_ref,
               xp, yc, *, tt, rb, cb):
    i = pl.program_id(1)

    @pl.when(i == 0)
    def _():
        xp[0:HALO, :] = cache_ref[0]

    @pl.when(i > 0)
    def _():
        xp[0:HALO, :] = xp[tt:tt + HALO, :]

    xp[HALO:HALO + tt, :] = av_ref[...] * _sigmoid(ag_ref[...])
    c_ch = av_ref.shape[1]
    first = HALO - (CONV_WIDTH - 1)
    for r0 in range(0, tt, rb):
        for c0 in range(0, c_ch, cb):
            acc = jnp.zeros((rb, cb), F32)
            for j in range(CONV_WIDTH):
                acc = acc + w_ref[j:j + 1, c0:c0 + cb] * xp[first + j + r0:first + j + r0 + rb, c0:c0 + cb]
            yc[r0:r0 + rb, c0:c0 + cb] = acc + b_ref[:, c0:c0 + cb]
    y = yc[...]
    mu = jnp.mean(y, axis=-1, keepdims=True)
    yd = y - mu
    var = jnp.mean(yd * yd, axis=-1, keepdims=True)
    z = yd * lax.rsqrt(var + EPS) * lg_ref[...] + lb_ref[...]
    z = z * _sigmoid(z)
    y_ref[...] = _dot(z.astype(BF16), pw_ref[...]).astype(y_ref.dtype)
    st_ref[0] = xp[tt:tt + HALO, :]


def _conv_call(u_conv, cache, w, b, lg, lb, pw, *, row_off, n_seq, t_len, tt, name):
    c_ch = pw.shape[0]
    tiles = t_len // tt
    blk0 = row_off // tt
    rb = min(64, tt)
    row_map = lambda s, i: (blk0 + s * tiles + i, 0)
    gate_map = lambda s, i: (blk0 + s * tiles + i, 1)
    out_map = lambda s, i: (s * tiles + i, 0)
    fixed = lambda s, i: (0, 0)
    seq_map = lambda s, i: (s, 0, 0)
    return pl.pallas_call(
        functools.partial(_conv_body, tt=tt, rb=rb, cb=256),
        grid=(n_seq, tiles),
        in_specs=[pl.BlockSpec((tt, c_ch), row_map), pl.BlockSpec((tt, c_ch), gate_map),
                  pl.BlockSpec((1, HALO, c_ch), seq_map),
                  pl.BlockSpec((HALO, c_ch), fixed), pl.BlockSpec((1, c_ch), fixed),
                  pl.BlockSpec((1, c_ch), fixed), pl.BlockSpec((1, c_ch), fixed),
                  pl.BlockSpec((c_ch, c_ch), fixed)],
        out_specs=[pl.BlockSpec((tt, c_ch), out_map), pl.BlockSpec((1, HALO, c_ch), seq_map)],
        out_shape=[jax.ShapeDtypeStruct((n_seq * t_len, c_ch), BF16),
                   jax.ShapeDtypeStruct((n_seq, HALO, c_ch), F32)],
        scratch_shapes=[pltpu.VMEM((HALO + tt, c_ch), F32), pltpu.VMEM((tt, c_ch), F32)],
        compiler_params=_cparams(("arbitrary", "arbitrary")), name=name,
    )(u_conv, u_conv, cache, w, b, lg, lb, pw)


def _fox_prompt_body(q_ref, k_ref, v_ref, cq_ref, ck_ref, o_ref, m_scr, l_scr, acc_scr, *, n_heads, tq, tk):
    qi = pl.program_id(0)
    ki = pl.program_id(1)
    scale = HEAD_DIM ** -0.5

    @pl.when(ki == 0)
    def _():
        m_scr[...] = jnp.full(m_scr.shape, -jnp.inf, F32)
        l_scr[...] = jnp.zeros(l_scr.shape, F32)
        acc_scr[...] = jnp.zeros(acc_scr.shape, F32)

    @pl.when(ki <= qi)
    def _():
        row = qi * tq + lax.broadcasted_iota(jnp.int32, (tq, tk), 0)
        col = ki * tk + lax.broadcasted_iota(jnp.int32, (tq, tk), 1)
        mask = col <= row
        for h in range(n_heads):
            hs = slice(h * HEAD_DIM, (h + 1) * HEAD_DIM)
            s = _dot_nt(q_ref[:, hs], k_ref[:, hs]) * scale
            s = s + cq_ref[:, h:h + 1] - ck_ref[h:h + 1, :]
            s = jnp.where(mask, s, -jnp.inf)
            m_prev = m_scr[h]
            m_new = jnp.maximum(m_prev, jnp.max(s, axis=-1, keepdims=True))
            alpha = jnp.exp(m_prev - m_new)
            p = jnp.exp(s - m_new)
            l_scr[h] = alpha * l_scr[h] + jnp.sum(p, axis=-1, keepdims=True)
            acc_scr[:, hs] = alpha * acc_scr[:, hs] + _dot(p.astype(BF16), v_ref[:, hs])
            m_scr[h] = m_new

    @pl.when(ki == qi)
    def _():
        for h in range(n_heads):
            hs = slice(h * HEAD_DIM, (h + 1) * HEAD_DIM)
            o_ref[:, hs] = (acc_scr[:, hs] / l_scr[h]).astype(o_ref.dtype)


def _fox_prompt_call(q, k, v, cq, ck, *, t_len, n_heads, name):
    tq = tk = 512
    w = n_heads * HEAD_DIM
    nq = t_len // tq
    q_map = lambda qi, ki: (qi, 0)
    k_map = lambda qi, ki: (jnp.minimum(ki, qi), 0)
    ck_map = lambda qi, ki: (0, jnp.minimum(ki, qi))
    return pl.pallas_call(
        functools.partial(_fox_prompt_body, n_heads=n_heads, tq=tq, tk=tk),
        grid=(nq, nq),
        in_specs=[pl.BlockSpec((tq, w), q_map), pl.BlockSpec((tk, w), k_map), pl.BlockSpec((tk, w), k_map),
                  pl.BlockSpec((tq, LANES), q_map), pl.BlockSpec((16, tk), ck_map)],
        out_specs=pl.BlockSpec((tq, w), q_map),
        out_shape=jax.ShapeDtypeStruct((t_len, w), BF16),
        scratch_shapes=[pltpu.VMEM((n_heads, tq, 1), F32), pltpu.VMEM((n_heads, tq, 1), F32),
                        pltpu.VMEM((tq, w), F32)],
        compiler_params=_cparams(("arbitrary", "arbitrary")), name=name,
    )(q, k, v, cq, ck)


def _fox_sample_body(q_ref, kn_ref, vn_ref, kc_ref, vc_ref, cq_ref, ck_ref, o_ref, *, n_heads, t_new, past):
    scale = HEAD_DIM ** -0.5
    row = lax.broadcasted_iota(jnp.int32, (t_new, t_new), 0)
    col = lax.broadcasted_iota(jnp.int32, (t_new, t_new), 1)
    mask = col <= row
    for h in range(n_heads):
        hs = slice(h * HEAD_DIM, (h + 1) * HEAD_DIM)
        qh = q_ref[:, hs]
        kc = kc_ref[0, :, hs].astype(BF16)
        vc = vc_ref[0, :, hs].astype(BF16)
        cq = cq_ref[:, h:h + 1]
        sp = _dot_nt(qh, kc) * scale + cq - ck_ref[0, h:h + 1, 0:past]
        sn = _dot_nt(qh, kn_ref[:, hs]) * scale + cq - ck_ref[0, h:h + 1, past:past + t_new]
        sn = jnp.where(mask, sn, -jnp.inf)
        m = jnp.maximum(jnp.max(sp, axis=-1, keepdims=True), jnp.max(sn, axis=-1, keepdims=True))
        pp = jnp.exp(sp - m)
        pn = jnp.exp(sn - m)
        l = jnp.sum(pp, axis=-1, keepdims=True) + jnp.sum(pn, axis=-1, keepdims=True)
        o = _dot(pp.astype(BF16), vc) + _dot(pn.astype(BF16), vn_ref[:, hs])
        o_ref[:, hs] = (o / l).astype(o_ref.dtype)


def _fox_sample_call(q, k, v, kc, vc, cq, ck, *, row_off, n_seq, t_new, past, n_heads, name):
    w = n_heads * HEAD_DIM
    blk0 = row_off // t_new
    new_map = lambda b: (blk0 + b, 0)
    cache_map = lambda b: (b, 0, 0)
    s_pad = ck.shape[-1]
    return pl.pallas_call(
        functools.partial(_fox_sample_body, n_heads=n_heads, t_new=t_new, past=past),
        grid=(n_seq,),
        in_specs=[pl.BlockSpec((t_new, w), new_map), pl.BlockSpec((t_new, w), new_map),
                  pl.BlockSpec((t_new, w), new_map),
                  pl.BlockSpec((1, past, w), cache_map), pl.BlockSpec((1, past, w), cache_map),
                  pl.BlockSpec((t_new, LANES), lambda b: (b, 0)),
                  pl.BlockSpec((1, n_heads, s_pad), cache_map)],
        out_specs=pl.BlockSpec((t_new, w), lambda b: (b, 0)),
        out_shape=jax.ShapeDtypeStruct((n_seq * t_new, w), BF16),
        compiler_params=_cparams(("arbitrary",)), name=name,
    )(q, k, v, kc, vc, cq, ck)


def _hgrn_body(q_ref, f_ref, i_ref, g_ref, lb_ref, gn_ref, s0_ref, y_ref, st_ref,
               st_scr, kp, bp, vp, *, chunk, n_heads):
    c = pl.program_id(1)
    nb = chunk // SUB

    @pl.when(c == 0)
    def _():
        for h in range(n_heads):
            st_scr[h] = s0_ref[0, h].T

    r = lax.broadcasted_iota(jnp.int32, (chunk, chunk), 0)
    cc = lax.broadcasted_iota(jnp.int32, (chunk, chunk), 1)
    tri = (cc <= r).astype(BF16)
    ones = jnp.ones((HEAD_DIM, HEAD_DIM), BF16)
    row16 = lax.broadcasted_iota(jnp.int32, (SUB, HEAD_DIM), 0)
    zpad = jnp.zeros((SUB, HEAD_DIM), F32)
    for h in range(n_heads):
        hs = slice(h * HEAD_DIM, (h + 1) * HEAD_DIM)
        q = q_ref[:, hs]
        v = i_ref[:, hs]
        lb = lb_ref[:, hs]
        f = lb + (1.0 - lb) * _sigmoid(f_ref[:, hs])
        logf = jnp.log(f)
        kk = 1.0 - f
        lh, lm, ll = _split3(logf)
        b = (_dot(tri, lh) + _dot(tri, lm)) + _dot(tri, ll)
        b_last = b[chunk - 1:chunk, :]
        st = st_scr[h]
        o_inter = _dot_nt((q * jnp.exp(b)).astype(BF16), st.astype(BF16))
        kp[h, 0:SUB, :] = zpad
        bp[h, 0:SUB, :] = zpad
        vp[h, 0:SUB, :] = zpad
        kp[h, SUB:SUB + chunk, :] = kk
        bp[h, SUB:SUB + chunk, :] = b
        vp[h, SUB:SUB + chunk, :] = v
        v16 = v.astype(BF16)
        outs = []
        for i in range(nb):
            lo, hi = i * SUB, (i + 1) * SUB
            q_i = q[lo:hi]
            b_i = b[lo:hi]
            o_i = o_inter[lo:hi]
            if i > 0:
                ref_b = b[lo - 1:lo, :]
                qd = (q_i * jnp.exp(b_i - ref_b)).astype(BF16)
                kd = (kk[0:lo] * jnp.exp(ref_b - b[0:lo])).astype(BF16)
                a = _dot_nt(qd, kd)
                o_i = o_i + _dot(a.astype(BF16), v16[0:lo])
            xs = []
            for d in range(SUB):
                ks = kp[h, SUB + lo - d:SUB + hi - d, :]
                if d == 0:
                    xs.append((q_i * ks).astype(BF16))
                else:
                    bs = bp[h, SUB + lo - d:SUB + hi - d, :]
                    dec = jnp.exp(jnp.where(row16 >= d, b_i - bs, -jnp.inf))
                    xs.append((q_i * ks * dec).astype(BF16))
            rr = _dot(jnp.concatenate(xs, axis=0), ones)
            for d in range(SUB):
                o_i = o_i + rr[d * SUB:(d + 1) * SUB] * vp[h, SUB + lo - d:SUB + hi - d, :]
            outs.append(o_i)
        o = jnp.concatenate(outs, axis=0)
        kd_all = (kk * jnp.exp(b_last - b)).astype(BF16)
        st_scr[h] = st * jnp.exp(b_last) + lax.dot_general(
            v16, kd_all, (((0,), (0,)), ((), ())), preferred_element_type=F32)
        ms = jnp.mean(o * o, axis=-1, keepdims=True)
        on = o * lax.rsqrt(ms + EPS) * gn_ref[...]
        hg = g_ref[:, hs]
        y_ref[:, hs] = (on * (hg * _sigmoid(hg))).astype(y_ref.dtype)

    @pl.when(c == pl.num_programs(1) - 1)
    def _():
        for h in range(n_heads):
            st_ref[0, h] = st_scr[h].T


def _hgrn_call(u_h, lb, gn, s0, *, row_off, n_seq, t_len, chunk, n_heads, name):
    w = n_heads * HEAD_DIM
    n_chunks = t_len // chunk
    blk0 = row_off // chunk

    def col(j):
        return lambda s, c: (blk0 + s * n_chunks + c, j)

    fixed = lambda s, c: (0, 0)
    st_map = lambda s, c: (s, 0, 0, 0)
    return pl.pallas_call(
        functools.partial(_hgrn_body, chunk=chunk, n_heads=n_heads),
        grid=(n_seq, n_chunks),
        in_specs=[pl.BlockSpec((chunk, w), col(0)), pl.BlockSpec((chunk, w), col(1)),
                  pl.BlockSpec((chunk, w), col(2)), pl.BlockSpec((chunk, w), col(3)),
                  pl.BlockSpec((1, w), fixed), pl.BlockSpec((1, HEAD_DIM), fixed),
                  pl.BlockSpec((1, n_heads, HEAD_DIM, HEAD_DIM), st_map)],
        out_specs=[pl.BlockSpec((chunk, w), lambda s, c: (s * n_chunks + c, 0)),
                   pl.BlockSpec((1, n_heads, HEAD_DIM, HEAD_DIM), st_map)],
        out_shape=[jax.ShapeDtypeStruct((n_seq * t_len, w), BF16),
                   jax.ShapeDtypeStruct((n_seq, n_heads, HEAD_DIM, HEAD_DIM), F32)],
        scratch_shapes=[pltpu.VMEM((n_heads, HEAD_DIM, HEAD_DIM), F32),
                        pltpu.VMEM((n_heads, SUB + chunk, HEAD_DIM), F32),
                        pltpu.VMEM((n_heads, SUB + chunk, HEAD_DIM), F32),
                        pltpu.VMEM((n_heads, SUB + chunk, HEAD_DIM), F32)],
        compiler_params=_cparams(("arbitrary", "arbitrary")), name=name,
    )(u_h, u_h, u_h, u_h, lb, gn, s0)


def _dispatch_body(tok_ref, nt_ref, x_ref, g_ref, xs_ref, buf, sem, *, tm):
    t = pl.program_id(0)

    @pl.when(t < nt_ref[0])
    def _():
        base = t * tm

        def _copy(r):
            return pltpu.make_async_copy(x_ref.at[pl.ds(tok_ref[base + r], 1), :],
                                         buf.at[pl.ds(r, 1), :], sem.at[0])

        def _issue(r, carry):
            _copy(r).start()
            return carry

        def _wait(r, carry):
            _copy(r).wait()
            return carry

        lax.fori_loop(0, tm, _issue, 0)
        lax.fori_loop(0, tm, _wait, 0)
        x = buf[...]
        ms = jnp.mean(x * x, axis=-1, keepdims=True)
        xs_ref[...] = (x * lax.rsqrt(ms + EPS) * g_ref[...]).astype(BF16)

    @pl.when(t >= nt_ref[0])
    def _():
        xs_ref[...] = jnp.zeros(xs_ref.shape, BF16)


def _dispatch_call(row_tok, nt, x, g, *, rows, name):
    n, d = x.shape
    tm = MOE_TM
    grid_spec = pltpu.PrefetchScalarGridSpec(
        num_scalar_prefetch=2, grid=(rows // tm,),
        in_specs=[pl.BlockSpec(memory_space=pl.ANY), pl.BlockSpec((1, d), lambda t, tok, nt: (0, 0))],
        out_specs=pl.BlockSpec((tm, d), lambda t, tok, nt: (t, 0)),
        scratch_shapes=[pltpu.VMEM((tm, d), F32), pltpu.SemaphoreType.DMA((1,))])
    return pl.pallas_call(
        functools.partial(_dispatch_body, tm=tm), grid_spec=grid_spec,
        out_shape=jax.ShapeDtypeStruct((rows, d), BF16),
        compiler_params=_cparams(("arbitrary",)), name=name,
    )(row_tok, nt, x, g.reshape(1, d))


def _weights_changed(te_ref, t):
    prev = te_ref[jnp.maximum(t - 1, 0)]
    return jnp.logical_or(t == 0, te_ref[t] != prev)


def _moe_up_body(te_ref, nt_ref, xs_ref, w1_ref, w3_ref, h_ref, w1b, w3b):
    t = pl.program_id(1)

    @pl.when(t < nt_ref[0])
    def _():
        @pl.when(_weights_changed(te_ref, t))
        def _():
            w1b[...] = w1_ref[...].astype(BF16)
            w3b[...] = w3_ref[...].astype(BF16)

        x = xs_ref[...]
        a = _dot(x, w1b[...])
        b = _dot(x, w3b[...])
        h_ref[...] = (a * _sigmoid(a) * b).astype(BF16)

    @pl.when(t >= nt_ref[0])
    def _():
        h_ref[...] = jnp.zeros(h_ref.shape, BF16)


def _moe_up_call(tile_e, nt, xs, w1, w3, *, tf, name):
    rows, d = xs.shape
    f = w1.shape[-1]
    tm = MOE_TM
    tile = lambda j, t, te, nt: jnp.minimum(t, nt[0] - 1)
    grid_spec = pltpu.PrefetchScalarGridSpec(
        num_scalar_prefetch=2, grid=(f // tf, rows // tm),
        in_specs=[pl.BlockSpec((tm, d), lambda j, t, te, nt: (tile(j, t, te, nt), 0)),
                  pl.BlockSpec((None, d, tf), lambda j, t, te, nt: (te[tile(j, t, te, nt)], 0, j)),
                  pl.BlockSpec((None, d, tf), lambda j, t, te, nt: (te[tile(j, t, te, nt)], 0, j))],
        out_specs=pl.BlockSpec((tm, tf), lambda j, t, te, nt: (t, j)),
        scratch_shapes=[pltpu.VMEM((d, tf), BF16), pltpu.VMEM((d, tf), BF16)])
    return pl.pallas_call(
        _moe_up_body, grid_spec=grid_spec, out_shape=jax.ShapeDtypeStruct((rows, f), BF16),
        compiler_params=_cparams(("arbitrary", "arbitrary")), name=name,
    )(tile_e, nt, xs, w1, w3)


def _moe_down_body(te_ref, nt_ref, h_ref, w2_ref, y_ref, w2b):
    t = pl.program_id(1)

    @pl.when(t < nt_ref[0])
    def _():
        @pl.when(_weights_changed(te_ref, t))
        def _():
            w2b[...] = w2_ref[...].astype(BF16)

        y_ref[...] = _dot(h_ref[...], w2b[...])

    @pl.when(t >= nt_ref[0])
    def _():
        y_ref[...] = jnp.zeros(y_ref.shape, F32)


def _moe_down_call(tile_e, nt, hmid, w2, *, tn, name):
    rows, f = hmid.shape
    d = w2.shape[-1]
    tm = MOE_TM
    tile = lambda j, t, te, nt: jnp.minimum(t, nt[0] - 1)
    grid_spec = pltpu.PrefetchScalarGridSpec(
        num_scalar_prefetch=2, grid=(d // tn, rows // tm),
        in_specs=[pl.BlockSpec((tm, f), lambda j, t, te, nt: (tile(j, t, te, nt), 0)),
                  pl.BlockSpec((None, f, tn), lambda j, t, te, nt: (te[tile(j, t, te, nt)], 0, j))],
        out_specs=pl.BlockSpec((tm, tn), lambda j, t, te, nt: (t, j)),
        scratch_shapes=[pltpu.VMEM((f, tn), BF16)])
    return pl.pallas_call(
        _moe_down_body, grid_spec=grid_spec, out_shape=jax.ShapeDtypeStruct((rows, d), F32),
        compiler_params=_cparams(("arbitrary", "arbitrary")), name=name,
    )(tile_e, nt, hmid, w2)


def _routing_tables(route, rows_pad):
    n = route.shape[0]
    e_flat = route[:, 0:2].astype(jnp.int32).reshape(-1)
    onehot = (e_flat[:, None] == jnp.arange(N_EXPERTS, dtype=jnp.int32)[None, :]).astype(jnp.int32)
    csum = jnp.cumsum(onehot, axis=0)
    rank = jnp.sum(onehot * csum, axis=1) - 1
    counts = csum[-1]
    tiles_per = (counts + MOE_TM - 1) // MOE_TM
    tile_end = jnp.cumsum(tiles_per)
    tile_start = tile_end - tiles_per
    n_tiles = tile_end[-1:].astype(jnp.int32)
    pos = (tile_start[e_flat] * MOE_TM + rank).astype(jnp.int32)
    row_tok = jnp.zeros((rows_pad,), jnp.int32).at[pos].set(jnp.arange(2 * n, dtype=jnp.int32) // 2)
    t_ids = jnp.arange(rows_pad // MOE_TM, dtype=jnp.int32)
    tile_e = jnp.minimum(jnp.sum((t_ids[:, None] >= tile_end[None, :]).astype(jnp.int32), axis=1),
                         N_EXPERTS - 1).astype(jnp.int32)
    return pos, row_tok, tile_e, n_tiles


def _pad_lanes(w):
    return jnp.pad(w, ((0, 0), (0, LANES - w.shape[1])))


def _split2_weights(w):
    hi = w.astype(BF16)
    mid = (w - hi.astype(F32)).astype(BF16)
    return hi, mid


@jax.jit
def _forward(x_prompt, x_sample, cache_conv, cache_fox_k, cache_fox_v, cache_fox_logf, state_hgrn,
             g_mix, w_in, conv_w, conv_b, conv_ln_g, conv_ln_b, conv_pw, fox_bf,
             hgrn_lb_logits, hgrn_gnorm, w_out, g_ffn, w_rg, b_rg, w_re, b_re, w1, w3, w2, g_final):
    bp, tp, d = x_prompt.shape
    bs, ts, _ = x_sample.shape
    depth = w_in.shape[0]
    c_ch = conv_pw.shape[-1]
    past = cache_fox_k.shape[2]
    n_heads = cache_fox_k.shape[3]
    fw = n_heads * HEAD_DIM
    assert bp == 1
    n_p = bp * tp
    n_s = bs * ts
    n = n_p + n_s
    rows_pad = 2 * n + N_EXPERTS * MOE_TM

    p_lb = jax.nn.softmax(hgrn_lb_logits.astype(F32), axis=0)
    lb_all = jnp.cumsum(p_lb, axis=0) - p_lb[0:1]

    x = jnp.concatenate([x_prompt.reshape(n_p, d), x_sample.reshape(n_s, d)], axis=0)
    combine = None
    outs_p = [[], [], [], [], []]
    outs_s = [[], [], [], [], []]
    s_tot = past + ts
    s_pad = -(-s_tot // CUM_CHUNK) * CUM_CHUNK

    for l in range(depth):
        wl = w_in[l]
        o = 0
        segs = []
        for size in (c_ch, c_ch, fw, fw, fw, n_heads, fw, fw, fw, fw):
            segs.append((o, o + size))
            o += size
        w_conv = wl[:, segs[0][0]:segs[1][1]].astype(BF16)
        w_q = wl[:, segs[2][0]:segs[2][1]].astype(BF16)
        w_k = wl[:, segs[3][0]:segs[3][1]].astype(BF16)
        w_v = wl[:, segs[4][0]:segs[4][1]].astype(BF16)
        w_ff = _pad_lanes(wl[:, segs[5][0]:segs[5][1]])
        w_h = wl[:, segs[6][0]:segs[9][1]].astype(BF16)
        ff_hi, ff_mid = _split2_weights(w_ff)
        ff_bias = _pad_lanes(fox_bf[l].reshape(1, n_heads).astype(F32))

        res = _norm_call(x, g_mix[l], combine=combine, small=(ff_hi, ff_mid, ff_bias), small_mode="logf",
                         name=f"norm_mix{l}")
        if combine is not None:
            x, hn, logf = res
        else:
            hn, logf = res

        (u_conv,) = _matmul([hn], w_conv, [F32], name=f"in_conv{l}")
        (q16,) = _matmul([hn], w_q, [BF16], name=f"in_q{l}")
        k32, k16 = _matmul([hn], w_k, [F32, BF16], name=f"in_k{l}")
        v32, v16 = _matmul([hn], w_v, [F32, BF16], name=f"in_v{l}")
        (u_h,) = _matmul([hn], w_h, [F32], name=f"in_h{l}")

        cw = jnp.pad(conv_w[l], ((0, HALO - CONV_WIDTH), (0, 0)))
        cargs = (cw, conv_b[l].reshape(1, c_ch), conv_ln_g[l].reshape(1, c_ch),
                 conv_ln_b[l].reshape(1, c_ch), conv_pw[l].astype(BF16))
        pad_rows = HALO - (CONV_WIDTH - 1)
        ya_p, cst_p = _conv_call(u_conv, jnp.zeros((bp, HALO, c_ch), F32), *cargs,
                                 row_off=0, n_seq=bp, t_len=tp, tt=256, name=f"conv_p{l}")
        cache_s = jnp.pad(cache_conv[l], ((0, 0), (pad_rows, 0), (0, 0)))
        ya_s, cst_s = _conv_call(u_conv, cache_s, *cargs,
                                 row_off=n_p, n_seq=bs, t_len=ts, tt=ts, name=f"conv_s{l}")

        lf_p = jnp.pad(logf[:n_p, :n_heads].T, ((0, 16 - n_heads), (0, 0)))
        c_p = _cumsum_rows(lf_p, name=f"cum_p{l}")
        cq_p = _pad_lanes(c_p.T)
        yb_p = _fox_prompt_call(q16, k16, v16, cq_p, c_p, t_len=tp, n_heads=n_heads, name=f"fox_p{l}")

        lf_new = logf[n_p:, :n_heads].reshape(bs, ts, n_heads).transpose(0, 2, 1)
        lf_past = cache_fox_logf[l].astype(F32).transpose(0, 2, 1)
        lf_s = jnp.concatenate([lf_past, lf_new], axis=-1).reshape(bs * n_heads, s_tot)
        lf_s = jnp.pad(lf_s, ((0, 0), (0, s_pad - s_tot)))
        c_s = _cumsum_rows(lf_s, name=f"cum_s{l}").reshape(bs, n_heads, s_pad)
        cq_s = _pad_lanes(c_s[:, :, past:s_tot].transpose(0, 2, 1).reshape(n_s, n_heads))
        yb_s = _fox_sample_call(q16, k16, v16, cache_fox_k[l].reshape(bs, past, fw),
                                cache_fox_v[l].reshape(bs, past, fw), cq_s, c_s,
                                row_off=n_p, n_seq=bs, t_new=ts, past=past, n_heads=n_heads,
                                name=f"fox_s{l}")

        lb = lb_all[l].reshape(1, fw)
        gn = hgrn_gnorm[l].reshape(1, HEAD_DIM).astype(F32)
        yc_p, st_p = _hgrn_call(u_h, lb, gn, jnp.zeros((bp, n_heads, HEAD_DIM, HEAD_DIM), F32),
                                row_off=0, n_seq=bp, t_len=tp, chunk=min(CHUNK, tp), n_heads=n_heads,
                                name=f"hgrn_p{l}")
        yc_s, st_s = _hgrn_call(u_h, lb, gn, state_hgrn[l].astype(F32),
                                row_off=n_p, n_seq=bs, t_len=ts, chunk=min(CHUNK, ts), n_heads=n_heads,
                                name=f"hgrn_s{l}")

        ya = jnp.concatenate([ya_p, ya_s], axis=0)
        yb = jnp.concatenate([yb_p, yb_s], axis=0)
        yc = jnp.concatenate([yc_p, yc_s], axis=0)
        (x,) = _matmul([ya, yb, yc], w_out[l].astype(BF16), [F32], residual=x, name=f"out_proj{l}")

        w_r = _pad_lanes(jnp.concatenate([w_rg[l], w_re[l].reshape(d, N_EXPERTS)], axis=1))
        r_hi, r_mid = _split2_weights(w_r)
        r_bias = _pad_lanes(jnp.concatenate([b_rg[l], b_re[l].reshape(-1)]).reshape(1, -1).astype(F32))
        (route,) = _norm_call(x, g_ffn[l], small=(r_hi, r_mid, r_bias), small_mode="route", want_hn=False,
                              name=f"router{l}")
        pos, row_tok, tile_e, n_tiles = _routing_tables(route, rows_pad)
        xs = _dispatch_call(row_tok, n_tiles, x, g_ffn[l], rows=rows_pad, name=f"dispatch{l}")
        f_dim = w1.shape[-1]
        hmid = _moe_up_call(tile_e, n_tiles, xs, w1[l].reshape(N_EXPERTS, d, f_dim),
                            w3[l].reshape(N_EXPERTS, d, f_dim), tf=256, name=f"moe_up{l}")
        ys = _moe_down_call(tile_e, n_tiles, hmid, w2[l].reshape(N_EXPERTS, f_dim, d), tn=1024,
                            name=f"moe_down{l}")
        combine = (pos, route, ys)

        k_p = k32[:n_p].reshape(bp, tp, n_heads, HEAD_DIM)
        v_p = v32[:n_p].reshape(bp, tp, n_heads, HEAD_DIM)
        k_s = k32[n_p:].reshape(bs, ts, n_heads, HEAD_DIM)
        v_s = v32[n_p:].reshape(bs, ts, n_heads, HEAD_DIM)
        lfo_p = logf[:n_p, :n_heads].reshape(bp, tp, n_heads)
        lfo_s = logf[n_p:, :n_heads].reshape(bs, ts, n_heads)
        for lst, val in zip(outs_p, (cst_p[:, pad_rows:], k_p, v_p, lfo_p, st_p)):
            lst.append(val)
        for lst, val in zip(outs_s, (cst_s[:, pad_rows:], k_s, v_s, lfo_s, st_s)):
            lst.append(val)

    _, y = _norm_call(x, g_final, combine=combine, want_hn=False, final=True, name="final_norm")
    y_prompt = y[:n_p].reshape(bp, tp, d)
    y_sample = y[n_p:].reshape(bs, ts, d)
    conv_p, k_p, v_p, logf_p, hgrn_p = [jnp.stack(o, axis=0) for o in outs_p]
    conv_s, k_s, v_s, logf_s, hgrn_s = [jnp.stack(o, axis=0) for o in outs_s]
    return (y_prompt, y_sample, conv_p, k_p, v_p, logf_p, hgrn_p, conv_s, k_s, v_s, logf_s, hgrn_s)


def kernel(x_prompt, x_sample, cache_conv, cache_fox_k, cache_fox_v, cache_fox_logf, state_hgrn, g_mix, w_in,
           conv_w, conv_b, conv_ln_g, conv_ln_b, conv_pw, fox_bf, hgrn_lb_logits, hgrn_gnorm, w_out, g_ffn,
           w_rg, b_rg, w_re, b_re, w1, w3, w2, g_final):
    return _forward(x_prompt, x_sample, cache_conv, cache_fox_k, cache_fox_v, cache_fox_logf, state_hgrn,
                    g_mix, w_in, conv_w, conv_b, conv_ln_g, conv_ln_b, conv_pw, fox_bf, hgrn_lb_logits,
                    hgrn_gnorm, w_out, g_ffn, w_rg, b_rg, w_re, b_re, w1, w3, w2, g_final)
```

```python
import functools
import math

import jax
import jax.numpy as jnp
from jax import lax
from jax.experimental import pallas as pl
from jax.experimental.pallas import tpu as pltpu

F32 = jnp.float32
BF16 = jnp.bfloat16

LANES = 128
HEAD_DIM = 128
CONV_WIDTH = 31
HALO = 32
SUB = 16
CHUNK = 64
N_GROUPS = 4
EXPERTS_PER_GROUP = 4
N_EXPERTS = N_GROUPS * EXPERTS_PER_GROUP
EPS = 1e-6
LOG2E = math.log2(math.e)
VMEM_LIMIT = 48 * 1024 * 1024

MOE_TM = 512
NORM_TM = 256
CUM_CHUNK = 256
PROJ_TM = 1024
PROJ_TN = 512
OUT_TM = 512
OUT_TN = 512
FOX_T = 512


def _cparams(sem):
    return pltpu.CompilerParams(dimension_semantics=sem, vmem_limit_bytes=VMEM_LIMIT)


def _sigmoid(x):
    return 1.0 / (1.0 + jnp.exp(-x))


def _split3(x):
    hi = x.astype(BF16)
    r1 = x - hi.astype(F32)
    mid = r1.astype(BF16)
    lo = (r1 - mid.astype(F32)).astype(BF16)
    return hi, mid, lo


def _dot(a, b):
    return jnp.dot(a, b, preferred_element_type=F32)


def _dot_nt(a, b):
    return lax.dot_general(a, b, (((1,), (1,)), ((), ())), preferred_element_type=F32)


def _proj_body(a_ref, wt_ref, *rest, mode, scale, np_tiles, gather_row0):
    if gather_row0 is None:
        out_refs = rest[:-1]
        w_scr = rest[-1]

        @pl.when(pl.program_id(1) == 0)
        def _():
            w_scr[...] = wt_ref[...].T.astype(BF16)
    else:
        out_refs = rest[:-3]
        w_scr, wbuf, sem = rest[-3:]
        tn = wbuf.shape[0]

        @pl.when(pl.program_id(1) == 0)
        def _():
            layer, row0 = gather_row0
            base = row0 + pl.program_id(0) * tn

            def _copy(r):
                return pltpu.make_async_copy(wt_ref.at[layer, pl.ds(base + r, 1), :],
                                             wbuf.at[pl.ds(r, 1), :], sem.at[0])

            def _issue(r, carry):
                _copy(r).start()
                return carry

            def _wait(r, carry):
                _copy(r).wait()
                return carry

            lax.fori_loop(0, tn, _issue, 0)
            lax.fori_loop(0, tn, _wait, 0)
            w_scr[...] = wbuf[...].T.astype(BF16)

    acc = _dot(a_ref[...], w_scr[...])
    if mode == "rows":
        out_refs[0][...] = acc.astype(out_refs[0].dtype)
        return
    if scale is not None:
        acc = acc * scale
    pieces = [acc[:, hh * HEAD_DIM:(hh + 1) * HEAD_DIM] for hh in range(acc.shape[1] // HEAD_DIM)]
    if mode == "heads":
        for hh, piece in enumerate(pieces):
            out_refs[0][hh] = piece.astype(out_refs[0].dtype)
        return
    p32_ref, s32_ref, b16_ref = out_refs
    for hh, piece in enumerate(pieces):
        b16_ref[hh] = piece.astype(BF16)

    @pl.when(pl.program_id(1) < np_tiles)
    def _():
        for hh, piece in enumerate(pieces):
            p32_ref[hh] = piece

    @pl.when(pl.program_id(1) >= np_tiles)
    def _():
        for hh, piece in enumerate(pieces):
            s32_ref[hh] = piece


def _proj_call(hn, wt, *, layer, row0, n_cols, mode, out_dtypes=None, scale=None, n_p=None, name):
    n, d = hn.shape
    tm, tn = PROJ_TM, PROJ_TN
    assert n % tm == 0 and n_cols % tn == 0
    scratch = [pltpu.VMEM((d, tn), BF16)]
    if row0 % tn == 0:
        gather_row0 = None
        w_spec = pl.BlockSpec((None, tn, d), lambda j, i: (layer, row0 // tn + j, 0))
    else:
        gather_row0 = (layer, row0)
        w_spec = pl.BlockSpec(memory_space=pl.ANY)
        scratch += [pltpu.VMEM((tn, d), F32), pltpu.SemaphoreType.DMA((1,))]
    np_tiles = None
    hpb = tn // HEAD_DIM
    n_hd = n_cols // HEAD_DIM
    head_blk = (hpb, tm, HEAD_DIM)
    if mode == "rows":
        out_shape = [jax.ShapeDtypeStruct((n, n_cols), out_dtypes[0])]
        out_specs = [pl.BlockSpec((tm, tn), lambda j, i: (i, j))]
    elif mode == "heads":
        out_shape = [jax.ShapeDtypeStruct((n_hd, n, HEAD_DIM), out_dtypes[0])]
        out_specs = [pl.BlockSpec(head_blk, lambda j, i: (j, i, 0))]
    else:
        assert n_p % tm == 0
        np_tiles = n_p // tm
        out_shape = [jax.ShapeDtypeStruct((n_hd, n_p, HEAD_DIM), F32),
                     jax.ShapeDtypeStruct((n_hd, n - n_p, HEAD_DIM), F32),
                     jax.ShapeDtypeStruct((n_hd, n, HEAD_DIM), BF16)]
        out_specs = [pl.BlockSpec(head_blk, lambda j, i: (j, jnp.minimum(i, np_tiles - 1), 0)),
                     pl.BlockSpec(head_blk, lambda j, i: (j, jnp.maximum(i - np_tiles, 0), 0)),
                     pl.BlockSpec(head_blk, lambda j, i: (j, i, 0))]
    return pl.pallas_call(
        functools.partial(_proj_body, mode=mode, scale=scale, np_tiles=np_tiles, gather_row0=gather_row0),
        grid=(n_cols // tn, n // tm),
        in_specs=[pl.BlockSpec((tm, d), lambda j, i: (i, 0)), w_spec],
        out_specs=out_specs, out_shape=out_shape,
        scratch_shapes=scratch,
        compiler_params=_cparams(("arbitrary", "arbitrary")), name=name,
    )(hn, wt)


def _out_proj_body(*refs, k_sizes, np_tiles, res_pair):
    n_a = len(k_sizes)
    ap_refs = refs[:n_a]
    as_refs = refs[n_a:2 * n_a]
    w_ref = refs[2 * n_a]
    pos = 2 * n_a + 1
    if res_pair:
        rp_ref, rs_ref = refs[pos], refs[pos + 1]
        pos += 2
    else:
        r_ref = refs[pos]
        pos += 1
    o_ref = refs[pos]
    w_scr = refs[pos + 1]
    i = pl.program_id(1)

    @pl.when(i == 0)
    def _():
        w_scr[...] = w_ref[...].astype(BF16)

    def run(a_refs, res):
        acc = res
        off = 0
        for a_ref, ks in zip(a_refs, k_sizes):
            acc = acc + _dot(a_ref[...], w_scr[off:off + ks, :])
            off += ks
        o_ref[...] = acc

    @pl.when(i < np_tiles)
    def _():
        run(ap_refs, rp_ref[...] if res_pair else r_ref[...])

    @pl.when(i >= np_tiles)
    def _():
        run(as_refs, rs_ref[...] if res_pair else r_ref[...])


def _out_proj_call(a_p, a_s, w_out, layer, res, *, name):
    n_p, n_s = a_p[0].shape[0], a_s[0].shape[0]
    k_sizes = tuple(a.shape[1] for a in a_p)
    d_in, d = w_out.shape[1], w_out.shape[2]
    tm, tn = OUT_TM, OUT_TN
    np_tiles, ns_tiles = n_p // tm, n_s // tm
    p_map = lambda j, i: (jnp.minimum(i, np_tiles - 1), 0)
    s_map = lambda j, i: (jnp.maximum(i - np_tiles, 0), 0)
    in_specs = [pl.BlockSpec((tm, ks), p_map) for ks in k_sizes]
    in_specs += [pl.BlockSpec((tm, ks), s_map) for ks in k_sizes]
    in_specs.append(pl.BlockSpec((None, d_in, tn), lambda j, i: (layer, 0, j)))
    res_pair = isinstance(res, tuple)
    if res_pair:
        in_specs += [pl.BlockSpec((tm, tn), lambda j, i: (jnp.minimum(i, np_tiles - 1), j)),
                     pl.BlockSpec((tm, tn), lambda j, i: (jnp.maximum(i - np_tiles, 0), j))]
        res_args = list(res)
    else:
        in_specs.append(pl.BlockSpec((tm, tn), lambda j, i: (i, j)))
        res_args = [res]
    return pl.pallas_call(
        functools.partial(_out_proj_body, k_sizes=k_sizes, np_tiles=np_tiles, res_pair=res_pair),
        grid=(d // tn, np_tiles + ns_tiles),
        in_specs=in_specs, out_specs=pl.BlockSpec((tm, tn), lambda j, i: (i, j)),
        out_shape=jax.ShapeDtypeStruct((n_p + n_s, d), F32),
        scratch_shapes=[pltpu.VMEM((d_in, tn), BF16)],
        compiler_params=_cparams(("arbitrary", "arbitrary")), name=name,
    )(*a_p, *a_s, w_out, *res_args)


def _log_sigmoid(z):
    return jnp.minimum(z, 0.0) - jnp.log1p(jnp.exp(-jnp.abs(z)))


def _route_from_logits(lg):
    g = [lg[:, i:i + 1] for i in range(N_GROUPS)]
    gmax = jnp.maximum(jnp.maximum(g[0], g[1]), jnp.maximum(g[2], g[3]))
    gidx = jnp.where(g[0] == gmax, 0, jnp.where(g[1] == gmax, 1, jnp.where(g[2] == gmax, 2, 3)))
    denom = (jnp.exp(g[0] - gmax) + jnp.exp(g[1] - gmax)) + (jnp.exp(g[2] - gmax) + jnp.exp(g[3] - gmax))
    p_group = 1.0 / denom
    e = []
    for j in range(EXPERTS_PER_GROUP):
        cols = [lg[:, N_GROUPS + gg * EXPERTS_PER_GROUP + j:N_GROUPS + gg * EXPERTS_PER_GROUP + j + 1]
                for gg in range(N_GROUPS)]
        e.append(jnp.where(gidx == 0, cols[0], jnp.where(gidx == 1, cols[1],
                                                         jnp.where(gidx == 2, cols[2], cols[3]))))
    v1 = jnp.maximum(jnp.maximum(e[0], e[1]), jnp.maximum(e[2], e[3]))
    i1 = jnp.where(e[0] == v1, 0, jnp.where(e[1] == v1, 1, jnp.where(e[2] == v1, 2, 3)))
    neg = jnp.float32(-jnp.inf)
    e2 = [jnp.where(i1 == j, neg, e[j]) for j in range(EXPERTS_PER_GROUP)]
    v2 = jnp.maximum(jnp.maximum(e2[0], e2[1]), jnp.maximum(e2[2], e2[3]))
    i2 = jnp.where(e2[0] == v2, 0, jnp.where(e2[1] == v2, 1, jnp.where(e2[2] == v2, 2, 3)))
    t = jnp.exp(v2 - v1)
    w1 = p_group / (1.0 + t)
    w2 = p_group * t / (1.0 + t)
    ex1 = (gidx * EXPERTS_PER_GROUP + i1).astype(F32)
    ex2 = (gidx * EXPERTS_PER_GROUP + i2).astype(F32)
    lane = lax.broadcasted_iota(jnp.int32, lg.shape, 1)
    return jnp.where(lane == 0, ex1, jnp.where(lane == 1, ex2,
                                               jnp.where(lane == 2, w1, jnp.where(lane == 3, w2, 0.0))))


def _norm_body(*refs, combine, pair, np_tiles, small_mode, want_x, want_hn, final, tm):
    refs = list(refs)
    pos_ref = refs.pop(0) if combine else None
    if pair:
        xp_ref = refs.pop(0)
        xs_ref = refs.pop(0)
    else:
        x_ref = refs.pop(0)
    if combine:
        route_ref = refs.pop(0)
        ys_ref = refs.pop(0)
    g_ref = refs.pop(0)
    if small_mode is not None:
        wh_ref = refs.pop(0)
        wm_ref = refs.pop(0)
        bias_ref = refs.pop(0)
    if want_x:
        xo_ref = refs.pop(0)
    if want_hn:
        hn_ref = refs.pop(0)
    if small_mode is not None:
        small_ref = refs.pop(0)
    if final:
        yp_ref = refs.pop(0)
        ysm_ref = refs.pop(0)
    if combine:
        gbuf = refs.pop(0)
        sem = refs.pop(0)

    i = pl.program_id(0)
    if pair:
        x = jnp.where(i < np_tiles, xp_ref[...], xs_ref[...])
    else:
        x = x_ref[...]
    if combine:
        slot = i % 2

        def _copy(tile, sl, r, k):
            row = pos_ref[2 * (tile * tm + r) + k]
            return pltpu.make_async_copy(ys_ref.at[pl.ds(row, 1), :], gbuf.at[sl, k, pl.ds(r, 1), :],
                                         sem.at[sl, k])

        def _issue(tile, sl):
            def body(r, carry):
                _copy(tile, sl, r, 0).start()
                _copy(tile, sl, r, 1).start()
                return carry
            lax.fori_loop(0, tm, body, 0)

        def _wait(tile, sl):
            def body(r, carry):
                _copy(tile, sl, r, 0).wait()
                _copy(tile, sl, r, 1).wait()
                return carry
            lax.fori_loop(0, tm, body, 0)

        @pl.when(i == 0)
        def _():
            _issue(0, 0)

        @pl.when(i + 1 < pl.num_programs(0))
        def _():
            _issue(i + 1, 1 - slot)

        _wait(i, slot)
        rt = route_ref[...]
        x = x + rt[:, 2:3] * gbuf[slot, 0] + rt[:, 3:4] * gbuf[slot, 1]
    if want_x:
        xo_ref[...] = x
    ms = jnp.mean(x * x, axis=-1, keepdims=True)
    hn = x * lax.rsqrt(ms + EPS) * g_ref[...]
    if final:
        @pl.when(i < np_tiles)
        def _():
            yp_ref[...] = hn

        @pl.when(i >= np_tiles)
        def _():
            ysm_ref[...] = hn
    if want_hn:
        hn_ref[...] = hn.astype(BF16)
    if small_mode is not None:
        hi = hn.astype(BF16)
        mid = (hn - hi.astype(F32)).astype(BF16)
        sm = _dot(hi, wh_ref[...]) + (_dot(hi, wm_ref[...]) + _dot(mid, wh_ref[...]))
        sm = sm + bias_ref[...]
        if small_mode == "logf":
            small_ref[...] = _log_sigmoid(sm)
        else:
            small_ref[...] = _route_from_logits(sm)


def _norm_call(x, g, *, n_p, combine=None, small=None, small_mode=None, want_hn=True, final=False, name="norm"):
    pair = isinstance(x, tuple)
    tm = NORM_TM
    np_tiles = n_p // tm
    if pair:
        n = x[0].shape[0] + x[1].shape[0]
        d = x[0].shape[1]
    else:
        n, d = x.shape
    row = lambda i, *_: (i, 0)
    fixed = lambda i, *_: (0, 0)
    p_map = lambda i, *_: (jnp.minimum(i, np_tiles - 1), 0)
    s_map = lambda i, *_: (jnp.maximum(i - np_tiles, 0), 0)
    if pair:
        in_specs = [pl.BlockSpec((tm, d), p_map), pl.BlockSpec((tm, d), s_map)]
        args = list(x)
    else:
        in_specs = [pl.BlockSpec((tm, d), row)]
        args = [x]
    scalar_args = []
    scratch = []
    if combine is not None:
        pos, route, ys = combine
        scalar_args = [pos]
        in_specs += [pl.BlockSpec((tm, LANES), row), pl.BlockSpec(memory_space=pl.ANY)]
        args += [route, ys]
        scratch = [pltpu.VMEM((2, 2, tm, d), F32), pltpu.SemaphoreType.DMA((2, 2))]
    in_specs.append(pl.BlockSpec((1, d), fixed))
    args.append(g.reshape(1, d))
    if small_mode is not None:
        wh, wm, bias = small
        in_specs += [pl.BlockSpec((d, LANES), fixed), pl.BlockSpec((d, LANES), fixed),
                     pl.BlockSpec((1, LANES), fixed)]
        args += [wh, wm, bias]
    want_x = combine is not None and not final
    out_shape, out_specs = [], []
    if want_x:
        out_shape.append(jax.ShapeDtypeStruct((n, d), F32))
        out_specs.append(pl.BlockSpec((tm, d), row))
    if want_hn:
        out_shape.append(jax.ShapeDtypeStruct((n, d), BF16))
        out_specs.append(pl.BlockSpec((tm, d), row))
    if small_mode is not None:
        out_shape.append(jax.ShapeDtypeStruct((n, LANES), F32))
        out_specs.append(pl.BlockSpec((tm, LANES), row))
    if final:
        out_shape += [jax.ShapeDtypeStruct((n_p, d), F32), jax.ShapeDtypeStruct((n - n_p, d), F32)]
        out_specs += [pl.BlockSpec((tm, d), p_map), pl.BlockSpec((tm, d), s_map)]
    grid_spec = pltpu.PrefetchScalarGridSpec(
        num_scalar_prefetch=len(scalar_args), grid=(n // tm,),
        in_specs=in_specs, out_specs=out_specs, scratch_shapes=scratch)
    return pl.pallas_call(
        functools.partial(_norm_body, combine=combine is not None, pair=pair, np_tiles=np_tiles,
                          small_mode=small_mode, want_x=want_x, want_hn=want_hn, final=final, tm=tm),
        grid_spec=grid_spec, out_shape=out_shape,
        compiler_params=_cparams(("arbitrary",)), name=name,
    )(*scalar_args, *args)


def _cumsum_body(x_ref, o_ref, *, n_chunks):
    r = lax.broadcasted_iota(jnp.int32, (CUM_CHUNK, CUM_CHUNK), 0)
    c = lax.broadcasted_iota(jnp.int32, (CUM_CHUNK, CUM_CHUNK), 1)
    tri = (r <= c).astype(BF16)
    carry = jnp.zeros((x_ref.shape[0], 1), F32)
    for ch in range(n_chunks):
        sl = slice(ch * CUM_CHUNK, (ch + 1) * CUM_CHUNK)
        hi, mid, lo = _split3(x_ref[:, sl])
        y = (_dot(hi, tri) + _dot(mid, tri)) + _dot(lo, tri) + carry
        o_ref[:, sl] = y
        carry = y[:, CUM_CHUNK - 1:CUM_CHUNK]


def _cumsum_rows(x, name):
    rows, s = x.shape
    assert s % CUM_CHUNK == 0
    return pl.pallas_call(
        functools.partial(_cumsum_body, n_chunks=s // CUM_CHUNK),
        out_shape=jax.ShapeDtypeStruct((rows, s), F32), name=name,
        compiler_params=pltpu.CompilerParams(vmem_limit_bytes=VMEM_LIMIT),
    )(x)


def _fox_aug_body(lf_ref, qa_ref, ka_ref, carry, *, n_heads):
    ch = pl.program_id(0)

    @pl.when(ch == 0)
    def _():
        carry[...] = jnp.zeros(carry.shape, F32)

    r = lax.broadcasted_iota(jnp.int32, (CUM_CHUNK, CUM_CHUNK), 0)
    c = lax.broadcasted_iota(jnp.int32, (CUM_CHUNK, CUM_CHUNK), 1)
    tri = (c <= r).astype(BF16)
    hi, mid, lo = _split3(lf_ref[...])
    cum = (_dot(tri, hi) + _dot(tri, mid)) + _dot(tri, lo) + carry[0:1, :]
    carry[0:1, :] = cum[CUM_CHUNK - 1:CUM_CHUNK, :]
    p1, p2, p3 = [p.astype(F32) for p in _split3(cum * LOG2E)]
    lane = lax.broadcasted_iota(jnp.int32, (CUM_CHUNK, LANES), 1)
    for h in range(n_heads):
        c1, c2, c3 = p1[:, h:h + 1], p2[:, h:h + 1], p3[:, h:h + 1]
        qa = jnp.where(lane == 0, c1, jnp.where(lane == 1, c2, jnp.where(lane == 2, c3,
                                                                         jnp.where(lane < 6, 1.0, 0.0))))
        ka = jnp.where(lane == 3, -c1, jnp.where(lane == 4, -c2, jnp.where(lane == 5, -c3,
                                                                           jnp.where(lane < 3, 1.0, 0.0))))
        qa_ref[h] = qa.astype(BF16)
        ka_ref[h] = ka.astype(BF16)


def _fox_aug_call(logf, *, t_len, n_heads, name):
    out = jax.ShapeDtypeStruct((n_heads, t_len, LANES), BF16)
    spec = pl.BlockSpec((n_heads, CUM_CHUNK, LANES), lambda c: (0, c, 0))
    return pl.pallas_call(
        functools.partial(_fox_aug_body, n_heads=n_heads),
        grid=(t_len // CUM_CHUNK,),
        in_specs=[pl.BlockSpec((CUM_CHUNK, LANES), lambda c: (c, 0))],
        out_specs=[spec, spec], out_shape=[out, out],
        scratch_shapes=[pltpu.VMEM((8, LANES), F32)],
        compiler_params=_cparams(("arbitrary",)), name=name,
    )(logf)


def _conv_body(av_ref, ag_ref, cache_ref, w_ref, b_ref, lg_ref, lb_ref, pw_ref, y_ref, st_ref,
               xp, yc, pw_scr, *, tt, rb, cb):
    i = pl.program_id(1)

    @pl.when(jnp.logical_and(i == 0, pl.program_id(0) == 0))
    def _():
        pw_scr[...] = pw_ref[...].astype(BF16)

    @pl.when(i == 0)
    def _():
        xp[0:HALO, :] = cache_ref[0]

    @pl.when(i > 0)
    def _():
        xp[0:HALO, :] = xp[tt:tt + HALO, :]

    xp[HALO:HALO + tt, :] = av_ref[...] * _sigmoid(ag_ref[...])
    c_ch = av_ref.shape[1]
    first = HALO - (CONV_WIDTH - 1)
    for r0 in range(0, tt, rb):
        for c0 in range(0, c_ch, cb):
            acc = jnp.zeros((rb, cb), F32)
            for j in range(CONV_WIDTH):
                acc = acc + w_ref[j:j + 1, c0:c0 + cb] * xp[first + j + r0:first + j + r0 + rb, c0:c0 + cb]
            yc[r0:r0 + rb, c0:c0 + cb] = acc + b_ref[:, c0:c0 + cb]
    y = yc[...]
    mu = jnp.mean(y, axis=-1, keepdims=True)
    yd = y - mu
    var = jnp.mean(yd * yd, axis=-1, keepdims=True)
    z = yd * lax.rsqrt(var + EPS) * lg_ref[...] + lb_ref[...]
    z = z * _sigmoid(z)
    y_ref[...] = _dot(z.astype(BF16), pw_scr[...]).astype(y_ref.dtype)
    st_ref[0] = xp[tt:tt + HALO, :]


def _conv_call(u_conv, cache, w, b, lg, lb, pw, layer, *, row_off, n_seq, t_len, tt, name):
    c_ch = pw.shape[-1]
    tiles = t_len // tt
    blk0 = row_off // tt
    rb = min(64, tt)
    row_map = lambda s, i: (blk0 + s * tiles + i, 0)
    gate_map = lambda s, i: (blk0 + s * tiles + i, 1)
    out_map = lambda s, i: (s * tiles + i, 0)
    fixed = lambda s, i: (0, 0)
    seq_map = lambda s, i: (s, 0, 0)
    return pl.pallas_call(
        functools.partial(_conv_body, tt=tt, rb=rb, cb=256),
        grid=(n_seq, tiles),
        in_specs=[pl.BlockSpec((tt, c_ch), row_map), pl.BlockSpec((tt, c_ch), gate_map),
                  pl.BlockSpec((1, HALO, c_ch), seq_map),
                  pl.BlockSpec((HALO, c_ch), fixed), pl.BlockSpec((1, c_ch), fixed),
                  pl.BlockSpec((1, c_ch), fixed), pl.BlockSpec((1, c_ch), fixed),
                  pl.BlockSpec((None, c_ch, c_ch), lambda s, i: (layer, 0, 0))],
        out_specs=[pl.BlockSpec((tt, c_ch), out_map), pl.BlockSpec((1, HALO, c_ch), seq_map)],
        out_shape=[jax.ShapeDtypeStruct((n_seq * t_len, c_ch), BF16),
                   jax.ShapeDtypeStruct((n_seq, HALO, c_ch), F32)],
        scratch_shapes=[pltpu.VMEM((HALO + tt, c_ch), F32), pltpu.VMEM((tt, c_ch), F32),
                        pltpu.VMEM((c_ch, c_ch), BF16)],
        compiler_params=_cparams(("arbitrary", "arbitrary")), name=name,
    )(u_conv, u_conv, cache, w, b, lg, lb, pw)


def _fox_prompt_body(q_ref, qa_ref, k_ref, ka_ref, v_ref, o_ref, m_scr, acc_scr, *, n_heads, tq, tk):
    qi = pl.program_id(0)
    ki = pl.program_id(1)

    @pl.when(ki == 0)
    def _():
        m_scr[...] = jnp.full(m_scr.shape, -jnp.inf, F32)
        acc_scr[...] = jnp.zeros(acc_scr.shape, F32)

    ones = jnp.ones((tk, HEAD_DIM), BF16)

    def step(masked):
        if masked:
            row = lax.broadcasted_iota(jnp.int32, (tq, tk), 0)
            col = lax.broadcasted_iota(jnp.int32, (tq, tk), 1)
            keep = col <= row
        for h in range(n_heads):
            qh = jnp.concatenate([q_ref[h], qa_ref[h]], axis=-1)
            kh = jnp.concatenate([k_ref[h], ka_ref[h]], axis=-1)
            s = _dot_nt(qh, kh)
            if masked:
                s = jnp.where(keep, s, -jnp.inf)
            m_prev = m_scr[h]
            m_new = jnp.maximum(m_prev, jnp.max(s, axis=-1, keepdims=True))
            alpha = jnp.exp2(m_prev - m_new)
            p = jnp.exp2(s - jnp.concatenate([m_new] * (tk // HEAD_DIM), axis=-1))
            vh = jnp.concatenate([v_ref[h], ones], axis=-1)
            acc_scr[h] = jnp.concatenate([alpha, alpha], axis=-1) * acc_scr[h] + _dot(p.astype(BF16), vh)
            m_scr[h] = m_new

    @pl.when(ki < qi)
    def _():
        step(False)

    @pl.when(ki == qi)
    def _():
        step(True)
        for h in range(n_heads):
            acc = acc_scr[h]
            o_ref[:, h * HEAD_DIM:(h + 1) * HEAD_DIM] = (acc[:, :HEAD_DIM] / acc[:, HEAD_DIM:]).astype(o_ref.dtype)


def _fox_prompt_call(q, k, v, qa, ka, *, t_len, n_heads, name):
    tq = tk = FOX_T
    w = n_heads * HEAD_DIM
    nq = t_len // tq
    q_map = lambda qi, ki: (0, qi, 0)
    k_map = lambda qi, ki: (0, jnp.minimum(ki, qi), 0)
    blk = (n_heads, tq, HEAD_DIM)
    return pl.pallas_call(
        functools.partial(_fox_prompt_body, n_heads=n_heads, tq=tq, tk=tk),
        grid=(nq, nq),
        in_specs=[pl.BlockSpec(blk, q_map), pl.BlockSpec(blk, q_map), pl.BlockSpec(blk, k_map),
                  pl.BlockSpec(blk, k_map), pl.BlockSpec(blk, k_map)],
        out_specs=pl.BlockSpec((tq, w), lambda qi, ki: (qi, 0)),
        out_shape=jax.ShapeDtypeStruct((t_len, w), BF16),
        scratch_shapes=[pltpu.VMEM((n_heads, tq, HEAD_DIM), F32),
                        pltpu.VMEM((n_heads, tq, 2 * HEAD_DIM), F32)],
        compiler_params=_cparams(("arbitrary", "arbitrary")), name=name,
    )(q, qa, k, ka, v)


def _fox_sample_body(q_ref, kn_ref, vn_ref, kc_ref, vc_ref, cq_ref, ck_ref, o_ref, *, n_heads, t_new, past):
    row = lax.broadcasted_iota(jnp.int32, (t_new, t_new), 0)
    col = lax.broadcasted_iota(jnp.int32, (t_new, t_new), 1)
    mask = col <= row
    for h in range(n_heads):
        qh = q_ref[h]
        kc = kc_ref[h].astype(BF16)
        vc = vc_ref[h].astype(BF16)
        cq = cq_ref[:, h:h + 1]
        sp = _dot_nt(qh, kc) + (cq - ck_ref[0, h:h + 1, 0:past]) * LOG2E
        sn = _dot_nt(qh, kn_ref[h]) + (cq - ck_ref[0, h:h + 1, past:past + t_new]) * LOG2E
        sn = jnp.where(mask, sn, -jnp.inf)
        m = jnp.maximum(jnp.max(sp, axis=-1, keepdims=True), jnp.max(sn, axis=-1, keepdims=True))
        pp = jnp.exp2(sp - m)
        pn = jnp.exp2(sn - m)
        l = jnp.sum(pp, axis=-1, keepdims=True) + jnp.sum(pn, axis=-1, keepdims=True)
        o = _dot(pp.astype(BF16), vc) + _dot(pn.astype(BF16), vn_ref[h])
        o_ref[:, h * HEAD_DIM:(h + 1) * HEAD_DIM] = (o / l).astype(o_ref.dtype)


def _fox_sample_call(q, k, v, kc, vc, cq, ck, layer, *, row_off, n_seq, t_new, past, n_heads, name):
    w = n_heads * HEAD_DIM
    blk0 = row_off // t_new
    new_map = lambda b: (0, blk0 + b, 0)
    cache_map = lambda b: (layer, b, 0, 0, 0)
    s_pad = ck.shape[-1]
    new_blk = (n_heads, t_new, HEAD_DIM)
    cache_blk = (None, None, n_heads, past, HEAD_DIM)
    return pl.pallas_call(
        functools.partial(_fox_sample_body, n_heads=n_heads, t_new=t_new, past=past),
        grid=(n_seq,),
        in_specs=[pl.BlockSpec(new_blk, new_map), pl.BlockSpec(new_blk, new_map), pl.BlockSpec(new_blk, new_map),
                  pl.BlockSpec(cache_blk, cache_map), pl.BlockSpec(cache_blk, cache_map),
                  pl.BlockSpec((t_new, LANES), lambda b: (b, 0)),
                  pl.BlockSpec((1, n_heads, s_pad), lambda b: (b, 0, 0))],
        out_specs=pl.BlockSpec((t_new, w), lambda b: (b, 0)),
        out_shape=jax.ShapeDtypeStruct((n_seq * t_new, w), BF16),
        compiler_params=_cparams(("arbitrary",)), name=name,
    )(q, k, v, kc, vc, cq, ck)


def _hgrn_body(q_ref, f_ref, i_ref, g_ref, lb_ref, gn_ref, s0_ref, y_ref, st_ref,
               st_scr, kp, bp, vp, *, chunk, n_heads):
    c = pl.program_id(1)
    nb = chunk // SUB

    @pl.when(c == 0)
    def _():
        for h in range(n_heads):
            st_scr[h] = s0_ref[0, h].T

    r = lax.broadcasted_iota(jnp.int32, (chunk, chunk), 0)
    cc = lax.broadcasted_iota(jnp.int32, (chunk, chunk), 1)
    tri = (cc <= r).astype(BF16)
    ones = jnp.ones((HEAD_DIM, HEAD_DIM), BF16)
    row16 = lax.broadcasted_iota(jnp.int32, (SUB, HEAD_DIM), 0)
    zpad = jnp.zeros((SUB, HEAD_DIM), F32)
    for h in range(n_heads):
        hs = slice(h * HEAD_DIM, (h + 1) * HEAD_DIM)
        q = q_ref[:, hs]
        v = i_ref[:, hs]
        lb = lb_ref[:, hs]
        f = lb + (1.0 - lb) * _sigmoid(f_ref[:, hs])
        logf = jnp.log(f)
        kk = 1.0 - f
        lh, lm, ll = _split3(logf)
        b = (_dot(tri, lh) + _dot(tri, lm)) + _dot(tri, ll)
        b_last = b[chunk - 1:chunk, :]
        st = st_scr[h]
        o_inter = _dot_nt((q * jnp.exp(b)).astype(BF16), st.astype(BF16))
        kp[h, 0:SUB, :] = zpad
        bp[h, 0:SUB, :] = zpad
        vp[h, 0:SUB, :] = zpad
        kp[h, SUB:SUB + chunk, :] = kk
        bp[h, SUB:SUB + chunk, :] = b
        vp[h, SUB:SUB + chunk, :] = v
        v16 = v.astype(BF16)
        outs = []
        for i in range(nb):
            lo, hi = i * SUB, (i + 1) * SUB
            q_i = q[lo:hi]
            b_i = b[lo:hi]
            o_i = o_inter[lo:hi]
            if i > 0:
                ref_b = b[lo - 1:lo, :]
                qd = (q_i * jnp.exp(b_i - ref_b)).astype(BF16)
                kd = (kk[0:lo] * jnp.exp(ref_b - b[0:lo])).astype(BF16)
                a = _dot_nt(qd, kd)
                o_i = o_i + _dot(a.astype(BF16), v16[0:lo])
            xs = []
            for d in range(SUB):
                ks = kp[h, SUB + lo - d:SUB + hi - d, :]
                if d == 0:
                    xs.append((q_i * ks).astype(BF16))
                else:
                    bs = bp[h, SUB + lo - d:SUB + hi - d, :]
                    dec = jnp.exp(jnp.where(row16 >= d, b_i - bs, -jnp.inf))
                    xs.append((q_i * ks * dec).astype(BF16))
            rr = _dot(jnp.concatenate(xs, axis=0), ones)
            for d in range(SUB):
                o_i = o_i + rr[d * SUB:(d + 1) * SUB] * vp[h, SUB + lo - d:SUB + hi - d, :]
            outs.append(o_i)
        o = jnp.concatenate(outs, axis=0)
        kd_all = (kk * jnp.exp(b_last - b)).astype(BF16)
        st_scr[h] = st * jnp.exp(b_last) + lax.dot_general(
            v16, kd_all, (((0,), (0,)), ((), ())), preferred_element_type=F32)
        ms = jnp.mean(o * o, axis=-1, keepdims=True)
        on = o * lax.rsqrt(ms + EPS) * gn_ref[...]
        hg = g_ref[:, hs]
        y_ref[:, hs] = (on * (hg * _sigmoid(hg))).astype(y_ref.dtype)

    @pl.when(c == pl.num_programs(1) - 1)
    def _():
        for h in range(n_heads):
            st_ref[0, h] = st_scr[h].T


def _hgrn_call(u_h, lb, gn, s0, layer, *, row_off, n_seq, t_len, chunk, n_heads, name):
    w = n_heads * HEAD_DIM
    n_chunks = t_len // chunk
    blk0 = row_off // chunk

    def col(j):
        return lambda s, c: (blk0 + s * n_chunks + c, j)

    fixed = lambda s, c: (0, 0)
    return pl.pallas_call(
        functools.partial(_hgrn_body, chunk=chunk, n_heads=n_heads),
        grid=(n_seq, n_chunks),
        in_specs=[pl.BlockSpec((chunk, w), col(0)), pl.BlockSpec((chunk, w), col(1)),
                  pl.BlockSpec((chunk, w), col(2)), pl.BlockSpec((chunk, w), col(3)),
                  pl.BlockSpec((1, w), fixed), pl.BlockSpec((1, HEAD_DIM), fixed),
                  pl.BlockSpec((None, 1, n_heads, HEAD_DIM, HEAD_DIM), lambda s, c: (layer, s, 0, 0, 0))],
        out_specs=[pl.BlockSpec((chunk, w), lambda s, c: (s * n_chunks + c, 0)),
                   pl.BlockSpec((1, n_heads, HEAD_DIM, HEAD_DIM), lambda s, c: (s, 0, 0, 0))],
        out_shape=[jax.ShapeDtypeStruct((n_seq * t_len, w), BF16),
                   jax.ShapeDtypeStruct((n_seq, n_heads, HEAD_DIM, HEAD_DIM), F32)],
        scratch_shapes=[pltpu.VMEM((n_heads, HEAD_DIM, HEAD_DIM), F32),
                        pltpu.VMEM((n_heads, SUB + chunk, HEAD_DIM), F32),
                        pltpu.VMEM((n_heads, SUB + chunk, HEAD_DIM), F32),
                        pltpu.VMEM((n_heads, SUB + chunk, HEAD_DIM), F32)],
        compiler_params=_cparams(("arbitrary", "arbitrary")), name=name,
    )(u_h, u_h, u_h, u_h, lb, gn, s0)


def _dispatch_body(tok_ref, nt_ref, x_ref, g_ref, xs_ref, buf, sem, *, tm):
    t = pl.program_id(0)
    nt = nt_ref[0]
    slot = t % 2

    def _copy(tile, sl, r):
        return pltpu.make_async_copy(x_ref.at[pl.ds(tok_ref[tile * tm + r], 1), :],
                                     buf.at[sl, pl.ds(r, 1), :], sem.at[sl])

    def _issue(tile, sl):
        def body(r, carry):
            _copy(tile, sl, r).start()
            return carry
        lax.fori_loop(0, tm, body, 0)

    def _wait(tile, sl):
        def body(r, carry):
            _copy(tile, sl, r).wait()
            return carry
        lax.fori_loop(0, tm, body, 0)

    @pl.when(t == 0)
    def _():
        _issue(0, 0)

    @pl.when(t + 1 < nt)
    def _():
        _issue(t + 1, 1 - slot)

    @pl.when(t < nt)
    def _():
        _wait(t, slot)
        x = buf[slot]
        ms = jnp.mean(x * x, axis=-1, keepdims=True)
        xs_ref[...] = (x * lax.rsqrt(ms + EPS) * g_ref[...]).astype(BF16)

    @pl.when(t >= nt)
    def _():
        xs_ref[...] = jnp.zeros(xs_ref.shape, BF16)


def _dispatch_call(row_tok, nt, x, g, *, rows, name):
    n, d = x.shape
    tm = MOE_TM
    grid_spec = pltpu.PrefetchScalarGridSpec(
        num_scalar_prefetch=2, grid=(rows // tm,),
        in_specs=[pl.BlockSpec(memory_space=pl.ANY), pl.BlockSpec((1, d), lambda t, tok, nt: (0, 0))],
        out_specs=pl.BlockSpec((tm, d), lambda t, tok, nt: (t, 0)),
        scratch_shapes=[pltpu.VMEM((2, tm, d), F32), pltpu.SemaphoreType.DMA((2,))])
    return pl.pallas_call(
        functools.partial(_dispatch_body, tm=tm), grid_spec=grid_spec,
        out_shape=jax.ShapeDtypeStruct((rows, d), BF16),
        compiler_params=_cparams(("arbitrary",)), name=name,
    )(row_tok, nt, x, g.reshape(1, d))


def _weights_changed(te_ref, t):
    prev = te_ref[jnp.maximum(t - 1, 0)]
    return jnp.logical_or(t == 0, te_ref[t] != prev)


def _moe_up_body(te_ref, nt_ref, xs_ref, w1_ref, w3_ref, h_ref, w1b, w3b):
    t = pl.program_id(1)

    @pl.when(t < nt_ref[0])
    def _():
        @pl.when(_weights_changed(te_ref, t))
        def _():
            w1b[...] = w1_ref[...].astype(BF16)
            w3b[...] = w3_ref[...].astype(BF16)

        x = xs_ref[...]
        a = _dot(x, w1b[...])
        b = _dot(x, w3b[...])
        h_ref[...] = (a * _sigmoid(a) * b).astype(BF16)

    @pl.when(t >= nt_ref[0])
    def _():
        h_ref[...] = jnp.zeros(h_ref.shape, BF16)


def _moe_up_call(tile_e, nt, xs, w1, w3, layer, *, tf, name):
    rows, d = xs.shape
    f = w1.shape[-1]
    tm = MOE_TM
    e0 = layer * N_EXPERTS
    tile = lambda j, t, te, nt: jnp.minimum(t, nt[0] - 1)
    w_map = lambda j, t, te, nt: (e0 + te[tile(j, t, te, nt)], 0, j)
    grid_spec = pltpu.PrefetchScalarGridSpec(
        num_scalar_prefetch=2, grid=(f // tf, rows // tm),
        in_specs=[pl.BlockSpec((tm, d), lambda j, t, te, nt: (tile(j, t, te, nt), 0)),
                  pl.BlockSpec((None, d, tf), w_map), pl.BlockSpec((None, d, tf), w_map)],
        out_specs=pl.BlockSpec((tm, tf), lambda j, t, te, nt: (t, j)),
        scratch_shapes=[pltpu.VMEM((d, tf), BF16), pltpu.VMEM((d, tf), BF16)])
    return pl.pallas_call(
        _moe_up_body, grid_spec=grid_spec, out_shape=jax.ShapeDtypeStruct((rows, f), BF16),
        compiler_params=_cparams(("arbitrary", "arbitrary")), name=name,
    )(tile_e, nt, xs, w1, w3)


def _moe_down_body(te_ref, nt_ref, h_ref, w2_ref, y_ref, w2b):
    t = pl.program_id(1)

    @pl.when(t < nt_ref[0])
    def _():
        @pl.when(_weights_changed(te_ref, t))
        def _():
            w2b[...] = w2_ref[...].astype(BF16)

        y_ref[...] = _dot(h_ref[...], w2b[...])

    @pl.when(t >= nt_ref[0])
    def _():
        y_ref[...] = jnp.zeros(y_ref.shape, F32)


def _moe_down_call(tile_e, nt, hmid, w2, layer, *, tn, name):
    rows, f = hmid.shape
    d = w2.shape[-1]
    tm = MOE_TM
    e0 = layer * N_EXPERTS
    tile = lambda j, t, te, nt: jnp.minimum(t, nt[0] - 1)
    grid_spec = pltpu.PrefetchScalarGridSpec(
        num_scalar_prefetch=2, grid=(d // tn, rows // tm),
        in_specs=[pl.BlockSpec((tm, f), lambda j, t, te, nt: (tile(j, t, te, nt), 0)),
                  pl.BlockSpec((None, f, tn), lambda j, t, te, nt: (e0 + te[tile(j, t, te, nt)], 0, j))],
        out_specs=pl.BlockSpec((tm, tn), lambda j, t, te, nt: (t, j)),
        scratch_shapes=[pltpu.VMEM((f, tn), BF16)])
    return pl.pallas_call(
        _moe_down_body, grid_spec=grid_spec, out_shape=jax.ShapeDtypeStruct((rows, d), F32),
        compiler_params=_cparams(("arbitrary", "arbitrary")), name=name,
    )(tile_e, nt, hmid, w2)


def _routing_tables(route, rows_pad):
    n = route.shape[0]
    e_flat = route[:, 0:2].astype(jnp.int32).reshape(-1)
    onehot = (e_flat[:, None] == jnp.arange(N_EXPERTS, dtype=jnp.int32)[None, :]).astype(jnp.int32)
    csum = jnp.cumsum(onehot, axis=0)
    rank = jnp.sum(onehot * csum, axis=1) - 1
    counts = csum[-1]
    tiles_per = (counts + MOE_TM - 1) // MOE_TM
    tile_end = jnp.cumsum(tiles_per)
    tile_start = tile_end - tiles_per
    n_tiles = tile_end[-1:].astype(jnp.int32)
    pos = (tile_start[e_flat] * MOE_TM + rank).astype(jnp.int32)
    row_tok = jnp.zeros((rows_pad,), jnp.int32).at[pos].set(jnp.arange(2 * n, dtype=jnp.int32) // 2)
    t_ids = jnp.arange(rows_pad // MOE_TM, dtype=jnp.int32)
    tile_e = jnp.minimum(jnp.sum((t_ids[:, None] >= tile_end[None, :]).astype(jnp.int32), axis=1),
                         N_EXPERTS - 1).astype(jnp.int32)
    return pos, row_tok, tile_e, n_tiles


def _pad_lanes(w):
    return jnp.pad(w, ((0, 0), (0, LANES - w.shape[1])))


def _split2_weights(w):
    hi = w.astype(BF16)
    mid = (w - hi.astype(F32)).astype(BF16)
    return hi, mid


@jax.jit
def _forward(x_prompt, x_sample, cache_conv, cache_fox_k, cache_fox_v, cache_fox_logf, state_hgrn,
             g_mix, w_in, conv_w, conv_b, conv_ln_g, conv_ln_b, conv_pw, fox_bf,
             hgrn_lb_logits, hgrn_gnorm, w_out, g_ffn, w_rg, b_rg, w_re, b_re, w1, w3, w2, g_final):
    bp, tp, d = x_prompt.shape
    bs, ts, _ = x_sample.shape
    depth = w_in.shape[0]
    c_ch = conv_pw.shape[-1]
    past = cache_fox_k.shape[2]
    n_heads = cache_fox_k.shape[3]
    fw = n_heads * HEAD_DIM
    f_dim = w1.shape[-1]
    assert bp == 1
    n_p = bp * tp
    n_s = bs * ts
    n = n_p + n_s
    rows_pad = 2 * n + N_EXPERTS * MOE_TM

    p_lb = jax.nn.softmax(hgrn_lb_logits.astype(F32), axis=0)
    lb_all = jnp.cumsum(p_lb, axis=0) - p_lb[0:1]

    ff0 = 2 * c_ch + 3 * fw
    wt_in = jnp.transpose(w_in, (0, 2, 1))
    wt_ff = wt_in[:, ff0:ff0 + n_heads]
    kc_all = jnp.transpose(cache_fox_k, (0, 1, 3, 2, 4))
    vc_all = jnp.transpose(cache_fox_v, (0, 1, 3, 2, 4))
    w1_all = w1.reshape(depth * N_EXPERTS, d, f_dim)
    w3_all = w3.reshape(depth * N_EXPERTS, d, f_dim)
    w2_all = w2.reshape(depth * N_EXPERTS, f_dim, d)
    s0_p = jnp.zeros((1, bp, n_heads, HEAD_DIM, HEAD_DIM), F32)

    o = 0
    segs = []
    for size in (c_ch, c_ch, fw, fw, fw, n_heads, fw, fw, fw, fw):
        segs.append(o)
        o += size
    q_scale = HEAD_DIM ** -0.5 * LOG2E

    x_p = x_prompt.reshape(n_p, d)
    x_s = x_sample.reshape(n_s, d)
    x = (x_p, x_s)
    combine = None
    outs_p = [[], [], [], [], []]
    outs_s = [[], [], [], [], []]
    s_tot = past + ts
    s_pad = -(-s_tot // CUM_CHUNK) * CUM_CHUNK
    pad_rows = HALO - (CONV_WIDTH - 1)

    for l in range(depth):
        w_ff = _pad_lanes(jnp.transpose(wt_ff[l]))
        ff_hi, ff_mid = _split2_weights(w_ff)
        ff_bias = _pad_lanes(fox_bf[l].reshape(1, n_heads).astype(F32))

        res = _norm_call(x, g_mix[l], n_p=n_p, combine=combine, small=(ff_hi, ff_mid, ff_bias),
                         small_mode="logf", name=f"norm_mix{l}")
        if combine is not None:
            x, hn, logf = res
        else:
            hn, logf = res

        (u_conv,) = _proj_call(hn, wt_in, layer=l, row0=segs[0], n_cols=2 * c_ch, mode="rows",
                               out_dtypes=[F32], name=f"in_conv{l}")
        (q16,) = _proj_call(hn, wt_in, layer=l, row0=segs[2], n_cols=fw, mode="heads",
                            out_dtypes=[BF16], scale=q_scale, name=f"in_q{l}")
        kp32, ks32, k16 = _proj_call(hn, wt_in, layer=l, row0=segs[3], n_cols=fw, mode="heads_split",
                                     n_p=n_p, name=f"in_k{l}")
        vp32, vs32, v16 = _proj_call(hn, wt_in, layer=l, row0=segs[4], n_cols=fw, mode="heads_split",
                                     n_p=n_p, name=f"in_v{l}")
        (u_h,) = _proj_call(hn, wt_in, layer=l, row0=ff0 + n_heads, n_cols=4 * fw, mode="rows",
                            out_dtypes=[F32], name=f"in_h{l}")

        cw = jnp.pad(conv_w[l], ((0, HALO - CONV_WIDTH), (0, 0)))
        cargs = (cw, conv_b[l].reshape(1, c_ch), conv_ln_g[l].reshape(1, c_ch),
                 conv_ln_b[l].reshape(1, c_ch), conv_pw, l)
        ya_p, cst_p = _conv_call(u_conv, jnp.zeros((bp, HALO, c_ch), F32), *cargs,
                                 row_off=0, n_seq=bp, t_len=tp, tt=256, name=f"conv_p{l}")
        cache_s = jnp.pad(cache_conv[l], ((0, 0), (pad_rows, 0), (0, 0)))
        ya_s, cst_s = _conv_call(u_conv, cache_s, *cargs,
                                 row_off=n_p, n_seq=bs, t_len=ts, tt=ts, name=f"conv_s{l}")

        qa, ka = _fox_aug_call(logf, t_len=tp, n_heads=n_heads, name=f"fox_aug{l}")
        yb_p = _fox_prompt_call(q16, k16, v16, qa, ka, t_len=tp, n_heads=n_heads, name=f"fox_p{l}")

        lf_new = logf[n_p:, :n_heads].reshape(bs, ts, n_heads).transpose(0, 2, 1)
        lf_past = cache_fox_logf[l].astype(F32).transpose(0, 2, 1)
        lf_s = jnp.concatenate([lf_past, lf_new], axis=-1).reshape(bs * n_heads, s_tot)
        lf_s = jnp.pad(lf_s, ((0, 0), (0, s_pad - s_tot)))
        c_s = _cumsum_rows(lf_s, name=f"cum_s{l}").reshape(bs, n_heads, s_pad)
        cq_s = _pad_lanes(c_s[:, :, past:s_tot].transpose(0, 2, 1).reshape(n_s, n_heads))
        yb_s = _fox_sample_call(q16, k16, v16, kc_all, vc_all, cq_s, c_s, l,
                                row_off=n_p, n_seq=bs, t_new=ts, past=past, n_heads=n_heads,
                                name=f"fox_s{l}")

        lb = lb_all[l].reshape(1, fw)
        gn = hgrn_gnorm[l].reshape(1, HEAD_DIM).astype(F32)
        yc_p, st_p = _hgrn_call(u_h, lb, gn, s0_p, 0,
                                row_off=0, n_seq=bp, t_len=tp, chunk=min(CHUNK, tp), n_heads=n_heads,
                                name=f"hgrn_p{l}")
        yc_s, st_s = _hgrn_call(u_h, lb, gn, state_hgrn, l,
                                row_off=n_p, n_seq=bs, t_len=ts, chunk=min(CHUNK, ts), n_heads=n_heads,
                                name=f"hgrn_s{l}")

        x = _out_proj_call([ya_p, yb_p, yc_p], [ya_s, yb_s, yc_s], w_out, l, x, name=f"out_proj{l}")

        w_r = _pad_lanes(jnp.concatenate([w_rg[l], w_re[l].reshape(d, N_EXPERTS)], axis=1))
        r_hi, r_mid = _split2_weights(w_r)
        r_bias = _pad_lanes(jnp.concatenate([b_rg[l], b_re[l].reshape(-1)]).reshape(1, -1).astype(F32))
        (route,) = _norm_call(x, g_ffn[l], n_p=n_p, small=(r_hi, r_mid, r_bias), small_mode="route",
                              want_hn=False, name=f"router{l}")
        pos, row_tok, tile_e, n_tiles = _routing_tables(route, rows_pad)
        xs = _dispatch_call(row_tok, n_tiles, x, g_ffn[l], rows=rows_pad, name=f"dispatch{l}")
        hmid = _moe_up_call(tile_e, n_tiles, xs, w1_all, w3_all, l, tf=256, name=f"moe_up{l}")
        ys = _moe_down_call(tile_e, n_tiles, hmid, w2_all, l, tn=2048, name=f"moe_down{l}")
        combine = (pos, route, ys)

        k_p = kp32.reshape(1, n_heads, bp, tp, HEAD_DIM)
        v_p = vp32.reshape(1, n_heads, bp, tp, HEAD_DIM)
        k_s = ks32.reshape(1, n_heads, bs, ts, HEAD_DIM)
        v_s = vs32.reshape(1, n_heads, bs, ts, HEAD_DIM)
        lfo_p = logf[:n_p, :n_heads].reshape(1, bp, tp, n_heads)
        lfo_s = logf[n_p:, :n_heads].reshape(1, bs, ts, n_heads)
        for lst, val in zip(outs_p, (cst_p[None, :, pad_rows:], k_p, v_p, lfo_p, st_p[None])):
            lst.append(val)
        for lst, val in zip(outs_s, (cst_s[None, :, pad_rows:], k_s, v_s, lfo_s, st_s[None])):
            lst.append(val)

    y_p, y_s = _norm_call(x, g_final, n_p=n_p, combine=combine, want_hn=False, final=True, name="final_norm")
    y_prompt = y_p.reshape(bp, tp, d)
    y_sample = y_s.reshape(bs, ts, d)
    conv_p, k_p, v_p, logf_p, hgrn_p = [jnp.concatenate(o, axis=0) for o in outs_p]
    conv_s, k_s, v_s, logf_s, hgrn_s = [jnp.concatenate(o, axis=0) for o in outs_s]
    head_last = lambda a: jnp.transpose(a, (0, 2, 3, 1, 4))
    return (y_prompt, y_sample, conv_p, head_last(k_p), head_last(v_p), logf_p, hgrn_p,
            conv_s, head_last(k_s), head_last(v_s), logf_s, hgrn_s)


def kernel(x_prompt, x_sample, cache_conv, cache_fox_k, cache_fox_v, cache_fox_logf, state_hgrn, g_mix, w_in,
           conv_w, conv_b, conv_ln_g, conv_ln_b, conv_pw, fox_bf, hgrn_lb_logits, hgrn_gnorm, w_out, g_ffn,
           w_rg, b_rg, w_re, b_re, w1, w3, w2, g_final):
    return _forward(x_prompt, x_sample, cache_conv, cache_fox_k, cache_fox_v, cache_fox_logf, state_hgrn,
                    g_mix, w_in, conv_w, conv_b, conv_ln_g, conv_ln_b, conv_pw, fox_bf, hgrn_lb_logits,
                    hgrn_gnorm, w_out, g_ffn, w_rg, b_rg, w_re, b_re, w1, w3, w2, g_final)
```

```python
import functools
import math

import jax
import jax.numpy as jnp
from jax import lax
from jax.experimental import pallas as pl
from jax.experimental.pallas import tpu as pltpu

F32 = jnp.float32
BF16 = jnp.bfloat16

LANES = 128
HEAD_DIM = 128
CONV_WIDTH = 31
HALO = 32
SUB = 16
CHUNK = 64
N_GROUPS = 4
EXPERTS_PER_GROUP = 4
N_EXPERTS = N_GROUPS * EXPERTS_PER_GROUP
EPS = 1e-6
LOG2E = math.log2(math.e)
VMEM_LIMIT = 48 * 1024 * 1024

MOE_TM = 512
NORM_TM = 256
CUM_CHUNK = 256
SUBLANES = 8
PROJ_TM = 512
PROJ_TN_WIDE = 1024
PROJ_TN_HEADS = 768
GATHER_UNROLL = 8
OUT_TM = 512
OUT_TN = 512
FOX_T = 512


def _cparams(sem):
    return pltpu.CompilerParams(dimension_semantics=sem, vmem_limit_bytes=VMEM_LIMIT)


def _sigmoid(x):
    return 1.0 / (1.0 + jnp.exp(-x))


def _split3(x):
    hi = x.astype(BF16)
    r1 = x - hi.astype(F32)
    mid = r1.astype(BF16)
    lo = (r1 - mid.astype(F32)).astype(BF16)
    return hi, mid, lo


def _dot(a, b):
    return jnp.dot(a, b, preferred_element_type=F32)


def _dot_nt(a, b):
    return lax.dot_general(a, b, (((1,), (1,)), ((), ())), preferred_element_type=F32)


def _proj_body(a_ref, wt_ref, *rest, mode, scale, np_tiles, gather_row0):
    if gather_row0 is None:
        out_refs = rest[:-1]
        w_scr = rest[-1]

        @pl.when(pl.program_id(1) == 0)
        def _():
            w_scr[...] = wt_ref[0].T.astype(BF16)
    else:
        out_refs = rest[:-3]
        w_scr, wbuf, sem = rest[-3:]
        tn = wbuf.shape[0]

        @pl.when(pl.program_id(1) == 0)
        def _():
            layer, row0 = gather_row0
            base = row0 + pl.program_id(0) * tn

            def _copy(r):
                return pltpu.make_async_copy(wt_ref.at[layer, pl.ds(base + r, 1), :],
                                             wbuf.at[pl.ds(r, 1), :], sem.at[0])

            def _issue(r, carry):
                _copy(r).start()
                return carry

            def _wait(r, carry):
                _copy(r).wait()
                return carry

            lax.fori_loop(0, tn, _issue, 0, unroll=GATHER_UNROLL)
            lax.fori_loop(0, tn, _wait, 0)
            w_scr[...] = wbuf[...].T.astype(BF16)

    acc = _dot(a_ref[...], w_scr[...])
    if mode == "rows":
        out_refs[0][...] = acc.astype(out_refs[0].dtype)
        return
    if scale is not None:
        acc = acc * scale
    pieces = [acc[:, hh * HEAD_DIM:(hh + 1) * HEAD_DIM] for hh in range(acc.shape[1] // HEAD_DIM)]
    if mode == "heads":
        for hh, piece in enumerate(pieces):
            out_refs[0][hh] = piece.astype(out_refs[0].dtype)
        return
    p32_ref, s32_ref, b16_ref = out_refs
    for hh, piece in enumerate(pieces):
        b16_ref[hh] = piece.astype(BF16)

    @pl.when(pl.program_id(1) < np_tiles)
    def _():
        for hh, piece in enumerate(pieces):
            p32_ref[hh] = piece

    @pl.when(pl.program_id(1) >= np_tiles)
    def _():
        for hh, piece in enumerate(pieces):
            s32_ref[hh] = piece


def _proj_call(hn, wt, *, layer, row0, n_cols, tn, mode, out_dtypes=None, scale=None, n_p=None, name):
    n, d = hn.shape
    tm = PROJ_TM
    assert n % tm == 0 and n_cols % tn == 0
    scratch = [pltpu.VMEM((d, tn), BF16)]
    if row0 % SUBLANES == 0:
        gather_row0 = None
        w_spec = pl.BlockSpec((pl.Element(1), pl.Element(tn), pl.Element(d)),
                              lambda j, i: (layer, pl.multiple_of(row0 + j * tn, SUBLANES), 0),
                              pipeline_mode=pl.Buffered(1))
    else:
        gather_row0 = (layer, row0)
        w_spec = pl.BlockSpec(memory_space=pl.ANY)
        scratch += [pltpu.VMEM((tn, d), F32), pltpu.SemaphoreType.DMA((1,))]
    np_tiles = None
    hpb = tn // HEAD_DIM
    n_hd = n_cols // HEAD_DIM
    head_blk = (hpb, tm, HEAD_DIM)
    if mode == "rows":
        out_shape = [jax.ShapeDtypeStruct((n, n_cols), out_dtypes[0])]
        out_specs = [pl.BlockSpec((tm, tn), lambda j, i: (i, j))]
    elif mode == "heads":
        out_shape = [jax.ShapeDtypeStruct((n_hd, n, HEAD_DIM), out_dtypes[0])]
        out_specs = [pl.BlockSpec(head_blk, lambda j, i: (j, i, 0))]
    else:
        assert n_p % tm == 0
        np_tiles = n_p // tm
        out_shape = [jax.ShapeDtypeStruct((n_hd, n_p, HEAD_DIM), F32),
                     jax.ShapeDtypeStruct((n_hd, n - n_p, HEAD_DIM), F32),
                     jax.ShapeDtypeStruct((n_hd, n, HEAD_DIM), BF16)]
        out_specs = [pl.BlockSpec(head_blk, lambda j, i: (j, jnp.minimum(i, np_tiles - 1), 0)),
                     pl.BlockSpec(head_blk, lambda j, i: (j, jnp.maximum(i - np_tiles, 0), 0)),
                     pl.BlockSpec(head_blk, lambda j, i: (j, i, 0))]
    return pl.pallas_call(
        functools.partial(_proj_body, mode=mode, scale=scale, np_tiles=np_tiles, gather_row0=gather_row0),
        grid=(n_cols // tn, n // tm),
        in_specs=[pl.BlockSpec((tm, d), lambda j, i: (i, 0)), w_spec],
        out_specs=out_specs, out_shape=out_shape,
        scratch_shapes=scratch,
        compiler_params=_cparams(("arbitrary", "arbitrary")), name=name,
    )(hn, wt)


def _out_proj_body(*refs, k_sizes, np_tiles, res_pair):
    n_a = len(k_sizes)
    ap_refs = refs[:n_a]
    as_refs = refs[n_a:2 * n_a]
    w_ref = refs[2 * n_a]
    pos = 2 * n_a + 1
    if res_pair:
        rp_ref, rs_ref = refs[pos], refs[pos + 1]
        pos += 2
    else:
        r_ref = refs[pos]
        pos += 1
    o_ref = refs[pos]
    w_scr = refs[pos + 1]
    i = pl.program_id(1)

    @pl.when(i == 0)
    def _():
        w_scr[...] = w_ref[...].astype(BF16)

    def run(a_refs, res):
        acc = res
        off = 0
        for a_ref, ks in zip(a_refs, k_sizes):
            acc = acc + _dot(a_ref[...], w_scr[off:off + ks, :])
            off += ks
        o_ref[...] = acc

    @pl.when(i < np_tiles)
    def _():
        run(ap_refs, rp_ref[...] if res_pair else r_ref[...])

    @pl.when(i >= np_tiles)
    def _():
        run(as_refs, rs_ref[...] if res_pair else r_ref[...])


def _out_proj_call(a_p, a_s, w_out, layer, res, *, name):
    n_p, n_s = a_p[0].shape[0], a_s[0].shape[0]
    k_sizes = tuple(a.shape[1] for a in a_p)
    d_in, d = w_out.shape[1], w_out.shape[2]
    tm, tn = OUT_TM, OUT_TN
    np_tiles, ns_tiles = n_p // tm, n_s // tm
    p_map = lambda j, i: (jnp.minimum(i, np_tiles - 1), 0)
    s_map = lambda j, i: (jnp.maximum(i - np_tiles, 0), 0)
    in_specs = [pl.BlockSpec((tm, ks), p_map) for ks in k_sizes]
    in_specs += [pl.BlockSpec((tm, ks), s_map) for ks in k_sizes]
    in_specs.append(pl.BlockSpec((None, d_in, tn), lambda j, i: (layer, 0, j)))
    res_pair = isinstance(res, tuple)
    if res_pair:
        in_specs += [pl.BlockSpec((tm, tn), lambda j, i: (jnp.minimum(i, np_tiles - 1), j)),
                     pl.BlockSpec((tm, tn), lambda j, i: (jnp.maximum(i - np_tiles, 0), j))]
        res_args = list(res)
    else:
        in_specs.append(pl.BlockSpec((tm, tn), lambda j, i: (i, j)))
        res_args = [res]
    return pl.pallas_call(
        functools.partial(_out_proj_body, k_sizes=k_sizes, np_tiles=np_tiles, res_pair=res_pair),
        grid=(d // tn, np_tiles + ns_tiles),
        in_specs=in_specs, out_specs=pl.BlockSpec((tm, tn), lambda j, i: (i, j)),
        out_shape=jax.ShapeDtypeStruct((n_p + n_s, d), F32),
        scratch_shapes=[pltpu.VMEM((d_in, tn), BF16)],
        compiler_params=_cparams(("arbitrary", "arbitrary")), name=name,
    )(*a_p, *a_s, w_out, *res_args)


def _log_sigmoid(z):
    return jnp.minimum(z, 0.0) - jnp.log1p(jnp.exp(-jnp.abs(z)))


def _route_from_logits(lg):
    g = [lg[:, i:i + 1] for i in range(N_GROUPS)]
    gmax = jnp.maximum(jnp.maximum(g[0], g[1]), jnp.maximum(g[2], g[3]))
    gidx = jnp.where(g[0] == gmax, 0, jnp.where(g[1] == gmax, 1, jnp.where(g[2] == gmax, 2, 3)))
    denom = (jnp.exp(g[0] - gmax) + jnp.exp(g[1] - gmax)) + (jnp.exp(g[2] - gmax) + jnp.exp(g[3] - gmax))
    p_group = 1.0 / denom
    e = []
    for j in range(EXPERTS_PER_GROUP):
        cols = [lg[:, N_GROUPS + gg * EXPERTS_PER_GROUP + j:N_GROUPS + gg * EXPERTS_PER_GROUP + j + 1]
                for gg in range(N_GROUPS)]
        e.append(jnp.where(gidx == 0, cols[0], jnp.where(gidx == 1, cols[1],
                                                         jnp.where(gidx == 2, cols[2], cols[3]))))
    v1 = jnp.maximum(jnp.maximum(e[0], e[1]), jnp.maximum(e[2], e[3]))
    i1 = jnp.where(e[0] == v1, 0, jnp.where(e[1] == v1, 1, jnp.where(e[2] == v1, 2, 3)))
    neg = jnp.float32(-jnp.inf)
    e2 = [jnp.where(i1 == j, neg, e[j]) for j in range(EXPERTS_PER_GROUP)]
    v2 = jnp.maximum(jnp.maximum(e2[0], e2[1]), jnp.maximum(e2[2], e2[3]))
    i2 = jnp.where(e2[0] == v2, 0, jnp.where(e2[1] == v2, 1, jnp.where(e2[2] == v2, 2, 3)))
    t = jnp.exp(v2 - v1)
    w1 = p_group / (1.0 + t)
    w2 = p_group * t / (1.0 + t)
    ex1 = (gidx * EXPERTS_PER_GROUP + i1).astype(F32)
    ex2 = (gidx * EXPERTS_PER_GROUP + i2).astype(F32)
    lane = lax.broadcasted_iota(jnp.int32, lg.shape, 1)
    return jnp.where(lane == 0, ex1, jnp.where(lane == 1, ex2,
                                               jnp.where(lane == 2, w1, jnp.where(lane == 3, w2, 0.0))))


def _norm_body(*refs, combine, pair, np_tiles, small_mode, want_x, want_hn, final, tm):
    refs = list(refs)
    pos_ref = refs.pop(0) if combine else None
    if pair:
        xp_ref = refs.pop(0)
        xs_ref = refs.pop(0)
    else:
        x_ref = refs.pop(0)
    if combine:
        route_ref = refs.pop(0)
        ys_ref = refs.pop(0)
    g_ref = refs.pop(0)
    if small_mode is not None:
        wh_ref = refs.pop(0)
        wm_ref = refs.pop(0)
        bias_ref = refs.pop(0)
    if want_x:
        xo_ref = refs.pop(0)
    if want_hn:
        hn_ref = refs.pop(0)
    if small_mode is not None:
        small_ref = refs.pop(0)
    if final:
        yp_ref = refs.pop(0)
        ysm_ref = refs.pop(0)
    if combine:
        gbuf = refs.pop(0)
        sem = refs.pop(0)

    i = pl.program_id(0)
    if pair:
        x = jnp.where(i < np_tiles, xp_ref[...], xs_ref[...])
    else:
        x = x_ref[...]
    if combine:
        slot = i % 2

        def _copy(tile, sl, r, k):
            row = pos_ref[2 * (tile * tm + r) + k]
            return pltpu.make_async_copy(ys_ref.at[pl.ds(row, 1), :], gbuf.at[sl, k, pl.ds(r, 1), :],
                                         sem.at[sl, k])

        def _issue(tile, sl):
            def body(r, carry):
                _copy(tile, sl, r, 0).start()
                _copy(tile, sl, r, 1).start()
                return carry
            lax.fori_loop(0, tm, body, 0, unroll=GATHER_UNROLL)

        def _wait(tile, sl):
            def body(r, carry):
                _copy(tile, sl, r, 0).wait()
                _copy(tile, sl, r, 1).wait()
                return carry
            lax.fori_loop(0, tm, body, 0)

        @pl.when(i == 0)
        def _():
            _issue(0, 0)

        @pl.when(i + 1 < pl.num_programs(0))
        def _():
            _issue(i + 1, 1 - slot)

        _wait(i, slot)
        rt = route_ref[...]
        x = x + rt[:, 2:3] * gbuf[slot, 0] + rt[:, 3:4] * gbuf[slot, 1]
    if want_x:
        xo_ref[...] = x
    ms = jnp.mean(x * x, axis=-1, keepdims=True)
    hn = x * lax.rsqrt(ms + EPS) * g_ref[...]
    if final:
        @pl.when(i < np_tiles)
        def _():
            yp_ref[...] = hn

        @pl.when(i >= np_tiles)
        def _():
            ysm_ref[...] = hn
    if want_hn:
        hn_ref[...] = hn.astype(BF16)
    if small_mode is not None:
        hi = hn.astype(BF16)
        mid = (hn - hi.astype(F32)).astype(BF16)
        sm = _dot(hi, wh_ref[...]) + (_dot(hi, wm_ref[...]) + _dot(mid, wh_ref[...]))
        sm = sm + bias_ref[...]
        if small_mode == "logf":
            small_ref[...] = _log_sigmoid(sm)
        else:
            small_ref[...] = _route_from_logits(sm)


def _norm_call(x, g, *, n_p, combine=None, small=None, small_mode=None, want_hn=True, final=False, name="norm"):
    pair = isinstance(x, tuple)
    tm = NORM_TM
    np_tiles = n_p // tm
    if pair:
        n = x[0].shape[0] + x[1].shape[0]
        d = x[0].shape[1]
    else:
        n, d = x.shape
    row = lambda i, *_: (i, 0)
    fixed = lambda i, *_: (0, 0)
    p_map = lambda i, *_: (jnp.minimum(i, np_tiles - 1), 0)
    s_map = lambda i, *_: (jnp.maximum(i - np_tiles, 0), 0)
    if pair:
        in_specs = [pl.BlockSpec((tm, d), p_map), pl.BlockSpec((tm, d), s_map)]
        args = list(x)
    else:
        in_specs = [pl.BlockSpec((tm, d), row)]
        args = [x]
    scalar_args = []
    scratch = []
    if combine is not None:
        pos, route, ys = combine
        scalar_args = [pos]
        in_specs += [pl.BlockSpec((tm, LANES), row), pl.BlockSpec(memory_space=pl.ANY)]
        args += [route, ys]
        scratch = [pltpu.VMEM((2, 2, tm, d), F32), pltpu.SemaphoreType.DMA((2, 2))]
    in_specs.append(pl.BlockSpec((1, d), fixed))
    args.append(g.reshape(1, d))
    if small_mode is not None:
        wh, wm, bias = small
        in_specs += [pl.BlockSpec((d, LANES), fixed), pl.BlockSpec((d, LANES), fixed),
                     pl.BlockSpec((1, LANES), fixed)]
        args += [wh, wm, bias]
    want_x = combine is not None and not final
    out_shape, out_specs = [], []
    if want_x:
        out_shape.append(jax.ShapeDtypeStruct((n, d), F32))
        out_specs.append(pl.BlockSpec((tm, d), row))
    if want_hn:
        out_shape.append(jax.ShapeDtypeStruct((n, d), BF16))
        out_specs.append(pl.BlockSpec((tm, d), row))
    if small_mode is not None:
        out_shape.append(jax.ShapeDtypeStruct((n, LANES), F32))
        out_specs.append(pl.BlockSpec((tm, LANES), row))
    if final:
        out_shape += [jax.ShapeDtypeStruct((n_p, d), F32), jax.ShapeDtypeStruct((n - n_p, d), F32)]
        out_specs += [pl.BlockSpec((tm, d), p_map), pl.BlockSpec((tm, d), s_map)]
    grid_spec = pltpu.PrefetchScalarGridSpec(
        num_scalar_prefetch=len(scalar_args), grid=(n // tm,),
        in_specs=in_specs, out_specs=out_specs, scratch_shapes=scratch)
    return pl.pallas_call(
        functools.partial(_norm_body, combine=combine is not None, pair=pair, np_tiles=np_tiles,
                          small_mode=small_mode, want_x=want_x, want_hn=want_hn, final=final, tm=tm),
        grid_spec=grid_spec, out_shape=out_shape,
        compiler_params=_cparams(("arbitrary",)), name=name,
    )(*scalar_args, *args)


def _cumsum_body(x_ref, o_ref, *, n_chunks):
    r = lax.broadcasted_iota(jnp.int32, (CUM_CHUNK, CUM_CHUNK), 0)
    c = lax.broadcasted_iota(jnp.int32, (CUM_CHUNK, CUM_CHUNK), 1)
    tri = (r <= c).astype(BF16)
    carry = jnp.zeros((x_ref.shape[0], 1), F32)
    for ch in range(n_chunks):
        sl = slice(ch * CUM_CHUNK, (ch + 1) * CUM_CHUNK)
        hi, mid, lo = _split3(x_ref[:, sl])
        y = (_dot(hi, tri) + _dot(mid, tri)) + _dot(lo, tri) + carry
        o_ref[:, sl] = y
        carry = y[:, CUM_CHUNK - 1:CUM_CHUNK]


def _cumsum_rows(x, name):
    rows, s = x.shape
    assert s % CUM_CHUNK == 0
    return pl.pallas_call(
        functools.partial(_cumsum_body, n_chunks=s // CUM_CHUNK),
        out_shape=jax.ShapeDtypeStruct((rows, s), F32), name=name,
        compiler_params=pltpu.CompilerParams(vmem_limit_bytes=VMEM_LIMIT),
    )(x)


def _fox_aug_body(lf_ref, qa_ref, ka_ref, carry, *, n_heads):
    ch = pl.program_id(0)

    @pl.when(ch == 0)
    def _():
        carry[...] = jnp.zeros(carry.shape, F32)

    r = lax.broadcasted_iota(jnp.int32, (CUM_CHUNK, CUM_CHUNK), 0)
    c = lax.broadcasted_iota(jnp.int32, (CUM_CHUNK, CUM_CHUNK), 1)
    tri = (c <= r).astype(BF16)
    hi, mid, lo = _split3(lf_ref[...])
    cum = (_dot(tri, hi) + _dot(tri, mid)) + _dot(tri, lo) + carry[0:1, :]
    carry[0:1, :] = cum[CUM_CHUNK - 1:CUM_CHUNK, :]
    p1, p2, p3 = [p.astype(F32) for p in _split3(cum * LOG2E)]
    lane = lax.broadcasted_iota(jnp.int32, (CUM_CHUNK, LANES), 1)
    for h in range(n_heads):
        c1, c2, c3 = p1[:, h:h + 1], p2[:, h:h + 1], p3[:, h:h + 1]
        qa = jnp.where(lane == 0, c1, jnp.where(lane == 1, c2, jnp.where(lane == 2, c3,
                                                                         jnp.where(lane < 6, 1.0, 0.0))))
        ka = jnp.where(lane == 3, -c1, jnp.where(lane == 4, -c2, jnp.where(lane == 5, -c3,
                                                                           jnp.where(lane < 3, 1.0, 0.0))))
        qa_ref[h] = qa.astype(BF16)
        ka_ref[h] = ka.astype(BF16)


def _fox_aug_call(logf, *, t_len, n_heads, name):
    out = jax.ShapeDtypeStruct((n_heads, t_len, LANES), BF16)
    spec = pl.BlockSpec((n_heads, CUM_CHUNK, LANES), lambda c: (0, c, 0))
    return pl.pallas_call(
        functools.partial(_fox_aug_body, n_heads=n_heads),
        grid=(t_len // CUM_CHUNK,),
        in_specs=[pl.BlockSpec((CUM_CHUNK, LANES), lambda c: (c, 0))],
        out_specs=[spec, spec], out_shape=[out, out],
        scratch_shapes=[pltpu.VMEM((8, LANES), F32)],
        compiler_params=_cparams(("arbitrary",)), name=name,
    )(logf)


def _conv_body(av_ref, ag_ref, cache_ref, w_ref, b_ref, lg_ref, lb_ref, pw_ref, y_ref, st_ref,
               xp, yc, pw_scr, *, tt, rb, cb):
    i = pl.program_id(1)

    @pl.when(jnp.logical_and(i == 0, pl.program_id(0) == 0))
    def _():
        pw_scr[...] = pw_ref[...].astype(BF16)

    @pl.when(i == 0)
    def _():
        xp[0:HALO, :] = cache_ref[0]

    @pl.when(i > 0)
    def _():
        xp[0:HALO, :] = xp[tt:tt + HALO, :]

    xp[HALO:HALO + tt, :] = av_ref[...] * _sigmoid(ag_ref[...])
    c_ch = av_ref.shape[1]
    first = HALO - (CONV_WIDTH - 1)
    for r0 in range(0, tt, rb):
        for c0 in range(0, c_ch, cb):
            acc = jnp.zeros((rb, cb), F32)
            for j in range(CONV_WIDTH):
                acc = acc + w_ref[j:j + 1, c0:c0 + cb] * xp[first + j + r0:first + j + r0 + rb, c0:c0 + cb]
            yc[r0:r0 + rb, c0:c0 + cb] = acc + b_ref[:, c0:c0 + cb]
    y = yc[...]
    mu = jnp.mean(y, axis=-1, keepdims=True)
    yd = y - mu
    var = jnp.mean(yd * yd, axis=-1, keepdims=True)
    z = yd * lax.rsqrt(var + EPS) * lg_ref[...] + lb_ref[...]
    z = z * _sigmoid(z)
    y_ref[...] = _dot(z.astype(BF16), pw_scr[...]).astype(y_ref.dtype)
    st_ref[0] = xp[tt:tt + HALO, :]


def _conv_call(u_conv, cache, w, b, lg, lb, pw, layer, *, row_off, n_seq, t_len, tt, name):
    c_ch = pw.shape[-1]
    tiles = t_len // tt
    blk0 = row_off // tt
    rb = min(64, tt)
    row_map = lambda s, i: (blk0 + s * tiles + i, 0)
    gate_map = lambda s, i: (blk0 + s * tiles + i, 1)
    out_map = lambda s, i: (s * tiles + i, 0)
    fixed = lambda s, i: (0, 0)
    seq_map = lambda s, i: (s, 0, 0)
    return pl.pallas_call(
        functools.partial(_conv_body, tt=tt, rb=rb, cb=256),
        grid=(n_seq, tiles),
        in_specs=[pl.BlockSpec((tt, c_ch), row_map), pl.BlockSpec((tt, c_ch), gate_map),
                  pl.BlockSpec((1, HALO, c_ch), seq_map),
                  pl.BlockSpec((HALO, c_ch), fixed), pl.BlockSpec((1, c_ch), fixed),
                  pl.BlockSpec((1, c_ch), fixed), pl.BlockSpec((1, c_ch), fixed),
                  pl.BlockSpec((None, c_ch, c_ch), lambda s, i: (layer, 0, 0))],
        out_specs=[pl.BlockSpec((tt, c_ch), out_map), pl.BlockSpec((1, HALO, c_ch), seq_map)],
        out_shape=[jax.ShapeDtypeStruct((n_seq * t_len, c_ch), BF16),
                   jax.ShapeDtypeStruct((n_seq, HALO, c_ch), F32)],
        scratch_shapes=[pltpu.VMEM((HALO + tt, c_ch), F32), pltpu.VMEM((tt, c_ch), F32),
                        pltpu.VMEM((c_ch, c_ch), BF16)],
        compiler_params=_cparams(("arbitrary", "arbitrary")), name=name,
    )(u_conv, u_conv, cache, w, b, lg, lb, pw)


def _fox_prompt_body(q_ref, qa_ref, k_ref, ka_ref, v_ref, o_ref, m_scr, acc_scr, *, n_heads, tq, tk):
    qi = pl.program_id(0)
    ki = pl.program_id(1)

    @pl.when(ki == 0)
    def _():
        m_scr[...] = jnp.full(m_scr.shape, -jnp.inf, F32)
        acc_scr[...] = jnp.zeros(acc_scr.shape, F32)

    ones = jnp.ones((tk, HEAD_DIM), BF16)

    def step(masked):
        if masked:
            row = lax.broadcasted_iota(jnp.int32, (tq, tk), 0)
            col = lax.broadcasted_iota(jnp.int32, (tq, tk), 1)
            keep = col <= row
        for h in range(n_heads):
            qh = jnp.concatenate([q_ref[h], qa_ref[h]], axis=-1)
            kh = jnp.concatenate([k_ref[h], ka_ref[h]], axis=-1)
            s = _dot_nt(qh, kh)
            if masked:
                s = jnp.where(keep, s, -jnp.inf)
            m_prev = m_scr[h]
            m_new = jnp.maximum(m_prev, jnp.max(s, axis=-1, keepdims=True))
            alpha = jnp.exp2(m_prev - m_new)
            p = jnp.exp2(s - jnp.concatenate([m_new] * (tk // HEAD_DIM), axis=-1))
            vh = jnp.concatenate([v_ref[h], ones], axis=-1)
            acc_scr[h] = jnp.concatenate([alpha, alpha], axis=-1) * acc_scr[h] + _dot(p.astype(BF16), vh)
            m_scr[h] = m_new

    @pl.when(ki < qi)
    def _():
        step(False)

    @pl.when(ki == qi)
    def _():
        step(True)
        for h in range(n_heads):
            acc = acc_scr[h]
            o_ref[:, h * HEAD_DIM:(h + 1) * HEAD_DIM] = (acc[:, :HEAD_DIM] / acc[:, HEAD_DIM:]).astype(o_ref.dtype)


def _fox_prompt_call(q, k, v, qa, ka, *, t_len, n_heads, name):
    tq = tk = FOX_T
    w = n_heads * HEAD_DIM
    nq = t_len // tq
    q_map = lambda qi, ki: (0, qi, 0)
    k_map = lambda qi, ki: (0, jnp.minimum(ki, qi), 0)
    blk = (n_heads, tq, HEAD_DIM)
    return pl.pallas_call(
        functools.partial(_fox_prompt_body, n_heads=n_heads, tq=tq, tk=tk),
        grid=(nq, nq),
        in_specs=[pl.BlockSpec(blk, q_map), pl.BlockSpec(blk, q_map), pl.BlockSpec(blk, k_map),
                  pl.BlockSpec(blk, k_map), pl.BlockSpec(blk, k_map)],
        out_specs=pl.BlockSpec((tq, w), lambda qi, ki: (qi, 0)),
        out_shape=jax.ShapeDtypeStruct((t_len, w), BF16),
        scratch_shapes=[pltpu.VMEM((n_heads, tq, HEAD_DIM), F32),
                        pltpu.VMEM((n_heads, tq, 2 * HEAD_DIM), F32)],
        compiler_params=_cparams(("arbitrary", "arbitrary")), name=name,
    )(q, qa, k, ka, v)


def _fox_sample_body(q_ref, kn_ref, vn_ref, kc_ref, vc_ref, cq_ref, ck_ref, o_ref, *, n_heads, t_new, past):
    row = lax.broadcasted_iota(jnp.int32, (t_new, t_new), 0)
    col = lax.broadcasted_iota(jnp.int32, (t_new, t_new), 1)
    mask = col <= row
    for h in range(n_heads):
        qh = q_ref[h]
        kc = kc_ref[h].astype(BF16)
        vc = vc_ref[h].astype(BF16)
        cq = cq_ref[:, h:h + 1]
        sp = _dot_nt(qh, kc) + (cq - ck_ref[0, h:h + 1, 0:past]) * LOG2E
        sn = _dot_nt(qh, kn_ref[h]) + (cq - ck_ref[0, h:h + 1, past:past + t_new]) * LOG2E
        sn = jnp.where(mask, sn, -jnp.inf)
        m = jnp.maximum(jnp.max(sp, axis=-1, keepdims=True), jnp.max(sn, axis=-1, keepdims=True))
        pp = jnp.exp2(sp - m)
        pn = jnp.exp2(sn - m)
        l = jnp.sum(pp, axis=-1, keepdims=True) + jnp.sum(pn, axis=-1, keepdims=True)
        o = _dot(pp.astype(BF16), vc) + _dot(pn.astype(BF16), vn_ref[h])
        o_ref[:, h * HEAD_DIM:(h + 1) * HEAD_DIM] = (o / l).astype(o_ref.dtype)


def _fox_sample_call(q, k, v, kc, vc, cq, ck, layer, *, row_off, n_seq, t_new, past, n_heads, name):
    w = n_heads * HEAD_DIM
    blk0 = row_off // t_new
    new_map = lambda b: (0, blk0 + b, 0)
    cache_map = lambda b: (layer, b, 0, 0, 0)
    s_pad = ck.shape[-1]
    new_blk = (n_heads, t_new, HEAD_DIM)
    cache_blk = (None, None, n_heads, past, HEAD_DIM)
    return pl.pallas_call(
        functools.partial(_fox_sample_body, n_heads=n_heads, t_new=t_new, past=past),
        grid=(n_seq,),
        in_specs=[pl.BlockSpec(new_blk, new_map), pl.BlockSpec(new_blk, new_map), pl.BlockSpec(new_blk, new_map),
                  pl.BlockSpec(cache_blk, cache_map), pl.BlockSpec(cache_blk, cache_map),
                  pl.BlockSpec((t_new, LANES), lambda b: (b, 0)),
                  pl.BlockSpec((1, n_heads, s_pad), lambda b: (b, 0, 0))],
        out_specs=pl.BlockSpec((t_new, w), lambda b: (b, 0)),
        out_shape=jax.ShapeDtypeStruct((n_seq * t_new, w), BF16),
        compiler_params=_cparams(("arbitrary",)), name=name,
    )(q, k, v, kc, vc, cq, ck)


def _hgrn_body(q_ref, f_ref, i_ref, g_ref, lb_ref, gn_ref, s0_ref, y_ref, st_ref,
               st_scr, kp, bp, vp, *, chunk, n_heads):
    c = pl.program_id(1)
    nb = chunk // SUB

    @pl.when(c == 0)
    def _():
        for h in range(n_heads):
            st_scr[h] = s0_ref[0, h].T

    r = lax.broadcasted_iota(jnp.int32, (chunk, chunk), 0)
    cc = lax.broadcasted_iota(jnp.int32, (chunk, chunk), 1)
    tri = (cc <= r).astype(BF16)
    ones = jnp.ones((HEAD_DIM, HEAD_DIM), BF16)
    row16 = lax.broadcasted_iota(jnp.int32, (SUB, HEAD_DIM), 0)
    zpad = jnp.zeros((SUB, HEAD_DIM), F32)
    for h in range(n_heads):
        hs = slice(h * HEAD_DIM, (h + 1) * HEAD_DIM)
        q = q_ref[:, hs]
        v = i_ref[:, hs]
        lb = lb_ref[:, hs]
        f = lb + (1.0 - lb) * _sigmoid(f_ref[:, hs])
        logf = jnp.log(f)
        kk = 1.0 - f
        lh, lm, ll = _split3(logf)
        b = (_dot(tri, lh) + _dot(tri, lm)) + _dot(tri, ll)
        b_last = b[chunk - 1:chunk, :]
        st = st_scr[h]
        o_inter = _dot_nt((q * jnp.exp(b)).astype(BF16), st.astype(BF16))
        kp[h, 0:SUB, :] = zpad
        bp[h, 0:SUB, :] = zpad
        vp[h, 0:SUB, :] = zpad
        kp[h, SUB:SUB + chunk, :] = kk
        bp[h, SUB:SUB + chunk, :] = b
        vp[h, SUB:SUB + chunk, :] = v
        v16 = v.astype(BF16)
        outs = []
        for i in range(nb):
            lo, hi = i * SUB, (i + 1) * SUB
            q_i = q[lo:hi]
            b_i = b[lo:hi]
            o_i = o_inter[lo:hi]
            if i > 0:
                ref_b = b[lo - 1:lo, :]
                qd = (q_i * jnp.exp(b_i - ref_b)).astype(BF16)
                kd = (kk[0:lo] * jnp.exp(ref_b - b[0:lo])).astype(BF16)
                a = _dot_nt(qd, kd)
                o_i = o_i + _dot(a.astype(BF16), v16[0:lo])
            xs = []
            for d in range(SUB):
                ks = kp[h, SUB + lo - d:SUB + hi - d, :]
                if d == 0:
                    xs.append((q_i * ks).astype(BF16))
                else:
                    bs = bp[h, SUB + lo - d:SUB + hi - d, :]
                    dec = jnp.exp(jnp.where(row16 >= d, b_i - bs, -jnp.inf))
                    xs.append((q_i * ks * dec).astype(BF16))
            rr = _dot(jnp.concatenate(xs, axis=0), ones)
            for d in range(SUB):
                o_i = o_i + rr[d * SUB:(d + 1) * SUB] * vp[h, SUB + lo - d:SUB + hi - d, :]
            outs.append(o_i)
        o = jnp.concatenate(outs, axis=0)
        kd_all = (kk * jnp.exp(b_last - b)).astype(BF16)
        st_scr[h] = st * jnp.exp(b_last) + lax.dot_general(
            v16, kd_all, (((0,), (0,)), ((), ())), preferred_element_type=F32)
        ms = jnp.mean(o * o, axis=-1, keepdims=True)
        on = o * lax.rsqrt(ms + EPS) * gn_ref[...]
        hg = g_ref[:, hs]
        y_ref[:, hs] = (on * (hg * _sigmoid(hg))).astype(y_ref.dtype)

    @pl.when(c == pl.num_programs(1) - 1)
    def _():
        for h in range(n_heads):
            st_ref[0, h] = st_scr[h].T


def _hgrn_call(u_h, lb, gn, s0, layer, *, row_off, n_seq, t_len, chunk, n_heads, name):
    w = n_heads * HEAD_DIM
    n_chunks = t_len // chunk
    blk0 = row_off // chunk

    def col(j):
        return lambda s, c: (blk0 + s * n_chunks + c, j)

    fixed = lambda s, c: (0, 0)
    return pl.pallas_call(
        functools.partial(_hgrn_body, chunk=chunk, n_heads=n_heads),
        grid=(n_seq, n_chunks),
        in_specs=[pl.BlockSpec((chunk, w), col(0)), pl.BlockSpec((chunk, w), col(1)),
                  pl.BlockSpec((chunk, w), col(2)), pl.BlockSpec((chunk, w), col(3)),
                  pl.BlockSpec((1, w), fixed), pl.BlockSpec((1, HEAD_DIM), fixed),
                  pl.BlockSpec((None, 1, n_heads, HEAD_DIM, HEAD_DIM), lambda s, c: (layer, s, 0, 0, 0))],
        out_specs=[pl.BlockSpec((chunk, w), lambda s, c: (s * n_chunks + c, 0)),
                   pl.BlockSpec((1, n_heads, HEAD_DIM, HEAD_DIM), lambda s, c: (s, 0, 0, 0))],
        out_shape=[jax.ShapeDtypeStruct((n_seq * t_len, w), BF16),
                   jax.ShapeDtypeStruct((n_seq, n_heads, HEAD_DIM, HEAD_DIM), F32)],
        scratch_shapes=[pltpu.VMEM((n_heads, HEAD_DIM, HEAD_DIM), F32),
                        pltpu.VMEM((n_heads, SUB + chunk, HEAD_DIM), F32),
                        pltpu.VMEM((n_heads, SUB + chunk, HEAD_DIM), F32),
                        pltpu.VMEM((n_heads, SUB + chunk, HEAD_DIM), F32)],
        compiler_params=_cparams(("arbitrary", "arbitrary")), name=name,
    )(u_h, u_h, u_h, u_h, lb, gn, s0)


def _dispatch_body(tok_ref, nt_ref, x_ref, g_ref, xs_ref, buf, sem, *, tm):
    t = pl.program_id(0)
    nt = nt_ref[0]
    slot = t % 2

    def _copy(tile, sl, r):
        return pltpu.make_async_copy(x_ref.at[pl.ds(tok_ref[tile * tm + r], 1), :],
                                     buf.at[sl, pl.ds(r, 1), :], sem.at[sl])

    def _issue(tile, sl):
        def body(r, carry):
            _copy(tile, sl, r).start()
            return carry
        lax.fori_loop(0, tm, body, 0, unroll=GATHER_UNROLL)

    def _wait(tile, sl):
        def body(r, carry):
            _copy(tile, sl, r).wait()
            return carry
        lax.fori_loop(0, tm, body, 0)

    @pl.when(t == 0)
    def _():
        _issue(0, 0)

    @pl.when(t + 1 < nt)
    def _():
        _issue(t + 1, 1 - slot)

    @pl.when(t < nt)
    def _():
        _wait(t, slot)
        x = buf[slot]
        ms = jnp.mean(x * x, axis=-1, keepdims=True)
        xs_ref[...] = (x * lax.rsqrt(ms + EPS) * g_ref[...]).astype(BF16)

    @pl.when(t >= nt)
    def _():
        xs_ref[...] = jnp.zeros(xs_ref.shape, BF16)


def _dispatch_call(row_tok, nt, x, g, *, rows, name):
    n, d = x.shape
    tm = MOE_TM
    grid_spec = pltpu.PrefetchScalarGridSpec(
        num_scalar_prefetch=2, grid=(rows // tm,),
        in_specs=[pl.BlockSpec(memory_space=pl.ANY), pl.BlockSpec((1, d), lambda t, tok, nt: (0, 0))],
        out_specs=pl.BlockSpec((tm, d), lambda t, tok, nt: (t, 0)),
        scratch_shapes=[pltpu.VMEM((2, tm, d), F32), pltpu.SemaphoreType.DMA((2,))])
    return pl.pallas_call(
        functools.partial(_dispatch_body, tm=tm), grid_spec=grid_spec,
        out_shape=jax.ShapeDtypeStruct((rows, d), BF16),
        compiler_params=_cparams(("arbitrary",)), name=name,
    )(row_tok, nt, x, g.reshape(1, d))


def _weights_changed(te_ref, t):
    prev = te_ref[jnp.maximum(t - 1, 0)]
    return jnp.logical_or(t == 0, te_ref[t] != prev)


def _moe_up_body(te_ref, nt_ref, xs_ref, w1_ref, w3_ref, h_ref, w1b, w3b):
    t = pl.program_id(1)

    @pl.when(t < nt_ref[0])
    def _():
        @pl.when(_weights_changed(te_ref, t))
        def _():
            w1b[...] = w1_ref[...].astype(BF16)
            w3b[...] = w3_ref[...].astype(BF16)

        x = xs_ref[...]
        a = _dot(x, w1b[...])
        b = _dot(x, w3b[...])
        h_ref[...] = (a * _sigmoid(a) * b).astype(BF16)

    @pl.when(t >= nt_ref[0])
    def _():
        h_ref[...] = jnp.zeros(h_ref.shape, BF16)


def _moe_up_call(tile_e, nt, xs, w1, w3, layer, *, tf, name):
    rows, d = xs.shape
    f = w1.shape[-1]
    tm = MOE_TM
    e0 = layer * N_EXPERTS
    tile = lambda j, t, te, nt: jnp.minimum(t, nt[0] - 1)
    w_map = lambda j, t, te, nt: (e0 + te[tile(j, t, te, nt)], 0, j)
    grid_spec = pltpu.PrefetchScalarGridSpec(
        num_scalar_prefetch=2, grid=(f // tf, rows // tm),
        in_specs=[pl.BlockSpec((tm, d), lambda j, t, te, nt: (tile(j, t, te, nt), 0)),
                  pl.BlockSpec((None, d, tf), w_map), pl.BlockSpec((None, d, tf), w_map)],
        out_specs=pl.BlockSpec((tm, tf), lambda j, t, te, nt: (t, j)),
        scratch_shapes=[pltpu.VMEM((d, tf), BF16), pltpu.VMEM((d, tf), BF16)])
    return pl.pallas_call(
        _moe_up_body, grid_spec=grid_spec, out_shape=jax.ShapeDtypeStruct((rows, f), BF16),
        compiler_params=_cparams(("arbitrary", "arbitrary")), name=name,
    )(tile_e, nt, xs, w1, w3)


def _moe_down_body(te_ref, nt_ref, h_ref, w2_ref, y_ref, w2b):
    t = pl.program_id(1)

    @pl.when(t < nt_ref[0])
    def _():
        @pl.when(_weights_changed(te_ref, t))
        def _():
            w2b[...] = w2_ref[...].astype(BF16)

        y_ref[...] = _dot(h_ref[...], w2b[...])

    @pl.when(t >= nt_ref[0])
    def _():
        y_ref[...] = jnp.zeros(y_ref.shape, F32)


def _moe_down_call(tile_e, nt, hmid, w2, layer, *, tn, name):
    rows, f = hmid.shape
    d = w2.shape[-1]
    tm = MOE_TM
    e0 = layer * N_EXPERTS
    tile = lambda j, t, te, nt: jnp.minimum(t, nt[0] - 1)
    grid_spec = pltpu.PrefetchScalarGridSpec(
        num_scalar_prefetch=2, grid=(d // tn, rows // tm),
        in_specs=[pl.BlockSpec((tm, f), lambda j, t, te, nt: (tile(j, t, te, nt), 0)),
                  pl.BlockSpec((None, f, tn), lambda j, t, te, nt: (e0 + te[tile(j, t, te, nt)], 0, j))],
        out_specs=pl.BlockSpec((tm, tn), lambda j, t, te, nt: (t, j)),
        scratch_shapes=[pltpu.VMEM((f, tn), BF16)])
    return pl.pallas_call(
        _moe_down_body, grid_spec=grid_spec, out_shape=jax.ShapeDtypeStruct((rows, d), F32),
        compiler_params=_cparams(("arbitrary", "arbitrary")), name=name,
    )(tile_e, nt, hmid, w2)


def _routing_tables(route, rows_pad):
    n = route.shape[0]
    e_flat = route[:, 0:2].astype(jnp.int32).reshape(-1)
    onehot = (e_flat[:, None] == jnp.arange(N_EXPERTS, dtype=jnp.int32)[None, :]).astype(jnp.int32)
    csum = jnp.cumsum(onehot, axis=0)
    rank = jnp.sum(onehot * csum, axis=1) - 1
    counts = csum[-1]
    tiles_per = (counts + MOE_TM - 1) // MOE_TM
    tile_end = jnp.cumsum(tiles_per)
    tile_start = tile_end - tiles_per
    n_tiles = tile_end[-1:].astype(jnp.int32)
    pos = (tile_start[e_flat] * MOE_TM + rank).astype(jnp.int32)
    row_tok = jnp.zeros((rows_pad,), jnp.int32).at[pos].set(jnp.arange(2 * n, dtype=jnp.int32) // 2)
    t_ids = jnp.arange(rows_pad // MOE_TM, dtype=jnp.int32)
    tile_e = jnp.minimum(jnp.sum((t_ids[:, None] >= tile_end[None, :]).astype(jnp.int32), axis=1),
                         N_EXPERTS - 1).astype(jnp.int32)
    return pos, row_tok, tile_e, n_tiles


def _pad_lanes(w):
    return jnp.pad(w, ((0, 0), (0, LANES - w.shape[1])))


def _split2_weights(w):
    hi = w.astype(BF16)
    mid = (w - hi.astype(F32)).astype(BF16)
    return hi, mid


@jax.jit
def _forward(x_prompt, x_sample, cache_conv, cache_fox_k, cache_fox_v, cache_fox_logf, state_hgrn,
             g_mix, w_in, conv_w, conv_b, conv_ln_g, conv_ln_b, conv_pw, fox_bf,
             hgrn_lb_logits, hgrn_gnorm, w_out, g_ffn, w_rg, b_rg, w_re, b_re, w1, w3, w2, g_final):
    bp, tp, d = x_prompt.shape
    bs, ts, _ = x_sample.shape
    depth = w_in.shape[0]
    c_ch = conv_pw.shape[-1]
    past = cache_fox_k.shape[2]
    n_heads = cache_fox_k.shape[3]
    fw = n_heads * HEAD_DIM
    f_dim = w1.shape[-1]
    assert bp == 1
    n_p = bp * tp
    n_s = bs * ts
    n = n_p + n_s
    rows_pad = 2 * n + N_EXPERTS * MOE_TM

    p_lb = jax.nn.softmax(hgrn_lb_logits.astype(F32), axis=0)
    lb_all = jnp.cumsum(p_lb, axis=0) - p_lb[0:1]

    ff0 = 2 * c_ch + 3 * fw
    wt_in = jnp.transpose(w_in, (0, 2, 1))
    wt_ff = wt_in[:, ff0:ff0 + n_heads]
    kc_all = jnp.transpose(cache_fox_k, (0, 1, 3, 2, 4))
    vc_all = jnp.transpose(cache_fox_v, (0, 1, 3, 2, 4))
    w1_all = w1.reshape(depth * N_EXPERTS, d, f_dim)
    w3_all = w3.reshape(depth * N_EXPERTS, d, f_dim)
    w2_all = w2.reshape(depth * N_EXPERTS, f_dim, d)
    s0_p = jnp.zeros((1, bp, n_heads, HEAD_DIM, HEAD_DIM), F32)

    o = 0
    segs = []
    for size in (c_ch, c_ch, fw, fw, fw, n_heads, fw, fw, fw, fw):
        segs.append(o)
        o += size
    q_scale = HEAD_DIM ** -0.5 * LOG2E

    x_p = x_prompt.reshape(n_p, d)
    x_s = x_sample.reshape(n_s, d)
    x = (x_p, x_s)
    combine = None
    outs_p = [[], [], [], [], []]
    outs_s = [[], [], [], [], []]
    s_tot = past + ts
    s_pad = -(-s_tot // CUM_CHUNK) * CUM_CHUNK
    pad_rows = HALO - (CONV_WIDTH - 1)

    for l in range(depth):
        w_ff = _pad_lanes(jnp.transpose(wt_ff[l]))
        ff_hi, ff_mid = _split2_weights(w_ff)
        ff_bias = _pad_lanes(fox_bf[l].reshape(1, n_heads).astype(F32))

        res = _norm_call(x, g_mix[l], n_p=n_p, combine=combine, small=(ff_hi, ff_mid, ff_bias),
                         small_mode="logf", name=f"norm_mix{l}")
        if combine is not None:
            x, hn, logf = res
        else:
            hn, logf = res

        (u_conv,) = _proj_call(hn, wt_in, layer=l, row0=segs[0], n_cols=2 * c_ch, tn=PROJ_TN_WIDE,
                               mode="rows", out_dtypes=[F32], name=f"in_conv{l}")
        (q16,) = _proj_call(hn, wt_in, layer=l, row0=segs[2], n_cols=fw, tn=PROJ_TN_HEADS, mode="heads",
                            out_dtypes=[BF16], scale=q_scale, name=f"in_q{l}")
        kp32, ks32, k16 = _proj_call(hn, wt_in, layer=l, row0=segs[3], n_cols=fw, tn=PROJ_TN_HEADS,
                                     mode="heads_split", n_p=n_p, name=f"in_k{l}")
        vp32, vs32, v16 = _proj_call(hn, wt_in, layer=l, row0=segs[4], n_cols=fw, tn=PROJ_TN_HEADS,
                                     mode="heads_split", n_p=n_p, name=f"in_v{l}")
        (u_h,) = _proj_call(hn, wt_in, layer=l, row0=ff0 + n_heads, n_cols=4 * fw, tn=PROJ_TN_WIDE,
                            mode="rows", out_dtypes=[F32], name=f"in_h{l}")

        cw = jnp.pad(conv_w[l], ((0, HALO - CONV_WIDTH), (0, 0)))
        cargs = (cw, conv_b[l].reshape(1, c_ch), conv_ln_g[l].reshape(1, c_ch),
                 conv_ln_b[l].reshape(1, c_ch), conv_pw, l)
        ya_p, cst_p = _conv_call(u_conv, jnp.zeros((bp, HALO, c_ch), F32), *cargs,
                                 row_off=0, n_seq=bp, t_len=tp, tt=256, name=f"conv_p{l}")
        cache_s = jnp.pad(cache_conv[l], ((0, 0), (pad_rows, 0), (0, 0)))
        ya_s, cst_s = _conv_call(u_conv, cache_s, *cargs,
                                 row_off=n_p, n_seq=bs, t_len=ts, tt=ts, name=f"conv_s{l}")

        qa, ka = _fox_aug_call(logf, t_len=tp, n_heads=n_heads, name=f"fox_aug{l}")
        yb_p = _fox_prompt_call(q16, k16, v16, qa, ka, t_len=tp, n_heads=n_heads, name=f"fox_p{l}")

        lf_new = logf[n_p:, :n_heads].reshape(bs, ts, n_heads).transpose(0, 2, 1)
        lf_past = cache_fox_logf[l].astype(F32).transpose(0, 2, 1)
        lf_s = jnp.concatenate([lf_past, lf_new], axis=-1).reshape(bs * n_heads, s_tot)
        lf_s = jnp.pad(lf_s, ((0, 0), (0, s_pad - s_tot)))
        c_s = _cumsum_rows(lf_s, name=f"cum_s{l}").reshape(bs, n_heads, s_pad)
        cq_s = _pad_lanes(c_s[:, :, past:s_tot].transpose(0, 2, 1).reshape(n_s, n_heads))
        yb_s = _fox_sample_call(q16, k16, v16, kc_all, vc_all, cq_s, c_s, l,
                                row_off=n_p, n_seq=bs, t_new=ts, past=past, n_heads=n_heads,
                                name=f"fox_s{l}")

        lb = lb_all[l].reshape(1, fw)
        gn = hgrn_gnorm[l].reshape(1, HEAD_DIM).astype(F32)
        yc_p, st_p = _hgrn_call(u_h, lb, gn, s0_p, 0,
                                row_off=0, n_seq=bp, t_len=tp, chunk=min(CHUNK, tp), n_heads=n_heads,
                                name=f"hgrn_p{l}")
        yc_s, st_s = _hgrn_call(u_h, lb, gn, state_hgrn, l,
                                row_off=n_p, n_seq=bs, t_len=ts, chunk=min(CHUNK, ts), n_heads=n_heads,
                                name=f"hgrn_s{l}")

        x = _out_proj_call([ya_p, yb_p, yc_p], [ya_s, yb_s, yc_s], w_out, l, x, name=f"out_proj{l}")

        w_r = _pad_lanes(jnp.concatenate([w_rg[l], w_re[l].reshape(d, N_EXPERTS)], axis=1))
        r_hi, r_mid = _split2_weights(w_r)
        r_bias = _pad_lanes(jnp.concatenate([b_rg[l], b_re[l].reshape(-1)]).reshape(1, -1).astype(F32))
        (route,) = _norm_call(x, g_ffn[l], n_p=n_p, small=(r_hi, r_mid, r_bias), small_mode="route",
                              want_hn=False, name=f"router{l}")
        pos, row_tok, tile_e, n_tiles = _routing_tables(route, rows_pad)
        xs = _dispatch_call(row_tok, n_tiles, x, g_ffn[l], rows=rows_pad, name=f"dispatch{l}")
        hmid = _moe_up_call(tile_e, n_tiles, xs, w1_all, w3_all, l, tf=256, name=f"moe_up{l}")
        ys = _moe_down_call(tile_e, n_tiles, hmid, w2_all, l, tn=2048, name=f"moe_down{l}")
        combine = (pos, route, ys)

        k_p = kp32.reshape(1, n_heads, bp, tp, HEAD_DIM)
        v_p = vp32.reshape(1, n_heads, bp, tp, HEAD_DIM)
        k_s = ks32.reshape(1, n_heads, bs, ts, HEAD_DIM)
        v_s = vs32.reshape(1, n_heads, bs, ts, HEAD_DIM)
        lfo_p = logf[:n_p, :n_heads].reshape(1, bp, tp, n_heads)
        lfo_s = logf[n_p:, :n_heads].reshape(1, bs, ts, n_heads)
        for lst, val in zip(outs_p, (cst_p[None, :, pad_rows:], k_p, v_p, lfo_p, st_p[None])):
            lst.append(val)
        for lst, val in zip(outs_s, (cst_s[None, :, pad_rows:], k_s, v_s, lfo_s, st_s[None])):
            lst.append(val)

    y_p, y_s = _norm_call(x, g_final, n_p=n_p, combine=combine, want_hn=False, final=True, name="final_norm")
    y_prompt = y_p.reshape(bp, tp, d)
    y_sample = y_s.reshape(bs, ts, d)
    conv_p, k_p, v_p, logf_p, hgrn_p = [jnp.concatenate(o, axis=0) for o in outs_p]
    conv_s, k_s, v_s, logf_s, hgrn_s = [jnp.concatenate(o, axis=0) for o in outs_s]
    head_last = lambda a: jnp.transpose(a, (0, 2, 3, 1, 4))
    return (y_prompt, y_sample, conv_p, head_last(k_p), head_last(v_p), logf_p, hgrn_p,
            conv_s, head_last(k_s), head_last(v_s), logf_s, hgrn_s)


def kernel(x_prompt, x_sample, cache_conv, cache_fox_k, cache_fox_v, cache_fox_logf, state_hgrn, g_mix, w_in,
           conv_w, conv_b, conv_ln_g, conv_ln_b, conv_pw, fox_bf, hgrn_lb_logits, hgrn_gnorm, w_out, g_ffn,
           w_rg, b_rg, w_re, b_re, w1, w3, w2, g_final):
    return _forward(x_prompt, x_sample, cache_conv, cache_fox_k, cache_fox_v, cache_fox_logf, state_hgrn,
                    g_mix, w_in, conv_w, conv_b, conv_ln_g, conv_ln_b, conv_pw, fox_bf, hgrn_lb_logits,
                    hgrn_gnorm, w_out, g_ffn, w_rg, b_rg, w_re, b_re, w1, w3, w2, g_final)
```
